```python
import jax, jax.numpy as jnp
from jax import lax
import numpy as np

D_MODEL = 1024
BATCH = 8
SEQ = 4096
DEPTH = 1

HEAD_DIM = 64
N_HEADS_A = 8
N_KV_A = 2
N_HEADS_B = 8
N_KV_B = 2
WIDTH_A = N_HEADS_A * HEAD_DIM
WIDTH_B = N_HEADS_B * HEAD_DIM
MIX_WIDTH = WIDTH_A + WIDTH_B
Q_BLOCK = 128
WINDOW = 128
GRID_W = 64
ROPE_THETA = 10000.0
EPS = 1e-6
D_FF = ((8 * D_MODEL + 3 * 256 - 1) // (3 * 256)) * 256
IN_COLS = (N_HEADS_A + 2 * N_KV_A + N_HEADS_B + 2 * N_KV_B) * HEAD_DIM
SPLITS = tuple(int(s) for s in np.cumsum([WIDTH_A, N_KV_A * HEAD_DIM, N_KV_A * HEAD_DIM,
                                           WIDTH_B, N_KV_B * HEAD_DIM])[:])

kernel_name = "hybrid_axial_global_windowed_sink_gqa_encoder"


def rms_norm(x, g):
    xf = x.astype(jnp.float32)
    y = xf * lax.rsqrt(jnp.mean(xf * xf, axis=-1, keepdims=True) + EPS)
    return (y * g.astype(jnp.float32)).astype(x.dtype)


def apply_rope(x, ang):
    cos = jnp.cos(ang)[None, :, None, :].astype(x.dtype)
    sin = jnp.sin(ang)[None, :, None, :].astype(x.dtype)
    x1, x2 = jnp.split(x, 2, axis=-1)
    return jnp.concatenate([x1 * cos - x2 * sin, x2 * cos + x1 * sin], axis=-1)


def global_attention(q, k, v):
    B, S, Hq, D = q.shape
    Hkv = k.shape[2]
    G = Hq // Hkv
    nb = S // Q_BLOCK
    qb = q.reshape(B, nb, Q_BLOCK, Hkv, G, D).transpose(1, 0, 2, 3, 4, 5)
    scale = D ** -0.5

    def one_block(qi):
        s = jnp.einsum('bqkgd,bskd->bkgqs', qi, k).astype(jnp.float32) * scale
        p = jax.nn.softmax(s, axis=-1).astype(v.dtype)
        return jnp.einsum('bkgqs,bskd->bqkgd', p, v)

    o = lax.map(one_block, qb)
    return o.transpose(1, 0, 2, 3, 4, 5).reshape(B, S, Hq * D)


def window_attention(q, k, v, sink):
    B, S, Hq, D = q.shape
    Hkv = k.shape[2]
    G = Hq // Hkv
    nb = S // Q_BLOCK
    C = 3 * Q_BLOCK
    qb = q.reshape(B, nb, Q_BLOCK, Hkv, G, D)
    pad = ((0, 0), (Q_BLOCK, Q_BLOCK), (0, 0), (0, 0))
    kp = jnp.pad(k, pad).reshape(B, nb + 2, Q_BLOCK, Hkv, D)
    vp = jnp.pad(v, pad).reshape(B, nb + 2, Q_BLOCK, Hkv, D)
    kb = jnp.concatenate([kp[:, :-2], kp[:, 1:-1], kp[:, 2:]], axis=2)
    vb = jnp.concatenate([vp[:, :-2], vp[:, 1:-1], vp[:, 2:]], axis=2)
    scale = D ** -0.5
    s = jnp.einsum('bnqkgd,bnckd->bnkgqc', qb, kb).astype(jnp.float32) * scale
    blk = jnp.arange(nb)[:, None, None]
    qpos = blk * Q_BLOCK + jnp.arange(Q_BLOCK)[None, :, None]
    kpos = (blk - 1) * Q_BLOCK + jnp.arange(C)[None, None, :]
    valid = (jnp.abs(kpos - qpos) <= WINDOW) & (kpos >= 0) & (kpos < S)
    s = jnp.where(valid[None, :, None, None], s, -jnp.inf)
    sink_l = sink.astype(jnp.float32).reshape(1, 1, Hkv, G, 1, 1)
    m = jnp.maximum(jnp.max(s, axis=-1, keepdims=True), sink_l)
    e = jnp.exp(s - m)
    den = jnp.sum(e, axis=-1, keepdims=True) + jnp.exp(sink_l - m)
    p = (e / den).astype(v.dtype)
    o = jnp.einsum('bnkgqc,bnckd->bnqkgd', p, vb)
    return o.reshape(B, S, Hq * D)


def setup_inputs(seed: int = 0) -> dict:
    key = jax.random.key(seed)
    ks = jax.random.split(key, 16)
    f32 = jnp.float32

    def gain(k, n):
        return 1.0 + 0.05 * jax.random.normal(k, (DEPTH, n), f32)

    def dense(k, fan_in, fan_out):
        return jax.random.normal(k, (DEPTH, fan_in, fan_out), f32) * fan_in ** -0.5

    return {
        "x": jax.random.normal(ks[0], (BATCH, SEQ, D_MODEL), f32),
        "norm_mix_pre": gain(ks[1], D_MODEL),
        "w_in": dense(ks[2], D_MODEL, IN_COLS),
        "q_norm_a": gain(ks[3], HEAD_DIM),
        "k_norm_a": gain(ks[4], HEAD_DIM),
        "sink_b": 0.5 * jax.random.normal(ks[5], (DEPTH, N_HEADS_B), f32),
        "group_norm_a": gain(ks[6], WIDTH_A),
        "group_norm_b": gain(ks[7], WIDTH_B),
        "w_out": dense(ks[8], MIX_WIDTH, D_MODEL),
        "norm_mix_post": gain(ks[9], D_MODEL),
        "norm_ffn_pre": gain(ks[10], D_MODEL),
        "w_gate": dense(ks[11], D_MODEL, D_FF),
        "w_up": dense(ks[12], D_MODEL, D_FF),
        "w_down": dense(ks[13], D_FF, D_MODEL),
        "norm_ffn_post": gain(ks[14], D_MODEL),
    }


def reference(x, norm_mix_pre, w_in, q_norm_a, k_norm_a, sink_b, group_norm_a, group_norm_b,
              w_out, norm_mix_post, norm_ffn_pre, w_gate, w_up, w_down, norm_ffn_post):
    B, S, _ = x.shape
    rows = S // GRID_W
    t = jnp.arange(S, dtype=jnp.float32)
    row = jnp.broadcast_to(jnp.arange(rows, dtype=jnp.float32)[:, None], (rows, GRID_W)).reshape(S)
    col = jnp.broadcast_to(jnp.arange(GRID_W, dtype=jnp.float32)[None, :], (rows, GRID_W)).reshape(S)
    ax_pairs = HEAD_DIM // 4
    freq_ax = ROPE_THETA ** (-jnp.arange(ax_pairs, dtype=jnp.float32) / ax_pairs)
    ang_axial = jnp.concatenate([row[:, None] * freq_ax[None, :], col[:, None] * freq_ax[None, :]], axis=-1)
    n_pairs = HEAD_DIM // 2
    freq_1d = ROPE_THETA ** (-jnp.arange(n_pairs, dtype=jnp.float32) / n_pairs)
    ang_1d = t[:, None] * freq_1d[None, :]

    for l in range(DEPTH):
        h = rms_norm(x, norm_mix_pre[l])
        proj = h @ w_in[l]
        qa, ka, va, qb, kb, vb = jnp.split(proj, SPLITS, axis=-1)

        qa = apply_rope(rms_norm(qa.reshape(B, S, N_HEADS_A, HEAD_DIM), q_norm_a[l]), ang_axial)
        ka = apply_rope(rms_norm(ka.reshape(B, S, N_KV_A, HEAD_DIM), k_norm_a[l]), ang_axial)
        va = va.reshape(B, S, N_KV_A, HEAD_DIM)
        out_a = global_attention(qa, ka, va)

        qb = apply_rope(qb.reshape(B, S, N_HEADS_B, HEAD_DIM), ang_1d)
        kb = apply_rope(kb.reshape(B, S, N_KV_B, HEAD_DIM), ang_1d)
        vb = vb.reshape(B, S, N_KV_B, HEAD_DIM)
        out_b = window_attention(qb, kb, vb, sink_b[l])

        mixed = jnp.concatenate([rms_norm(out_a, group_norm_a[l]),
                                 rms_norm(out_b, group_norm_b[l])], axis=-1) @ w_out[l]
        x = x + rms_norm(mixed, norm_mix_post[l])

        h = rms_norm(x, norm_ffn_pre[l])
        f = (jax.nn.silu(h @ w_gate[l]) * (h @ w_up[l])) @ w_down[l]
        x = x + rms_norm(f, norm_ffn_post[l])
    return x
```

```python
import functools
import math

import numpy as np
import jax
import jax.numpy as jnp
from jax import lax
from jax.experimental import pallas as pl
from jax.experimental.pallas import tpu as pltpu

D_MODEL = 1024
HEAD_DIM = 64
N_HEADS = 8
N_KV = 2
GROUP = N_HEADS // N_KV
WIDTH = N_HEADS * HEAD_DIM
KV_WIDTH = N_KV * HEAD_DIM
WINDOW = 128
GRID_W = 64
ROPE_THETA = 10000.0
EPS = 1e-6
D_FF = 2816
LOG2E = math.log2(math.e)
SCORE_SCALE = HEAD_DIM ** -0.5 * LOG2E

LANES = 128
ONES_ROWS = 16
VT_ROWS = HEAD_DIM + ONES_ROWS
ROW_TILE = 512
Q_TILE = 128
KEY_CHUNK = 512
FF_CHUNK = 256
VMEM_LIMIT = 56 * 1024 * 1024

_Q_HEAD_ORDER = tuple(h for j in range(GROUP) for h in (j, j + GROUP))


def _rms(x, gain):
    return x * lax.rsqrt(jnp.mean(x * x, axis=-1, keepdims=True) + EPS) * gain


def _rope_block(t, cos, sin_signed):
    lane = lax.broadcasted_iota(jnp.int32, t.shape, 1)
    first_half = (lane % HEAD_DIM) < (HEAD_DIM // 2)
    rot = jnp.where(first_half, pltpu.roll(t, LANES - HEAD_DIM // 2, 1), pltpu.roll(t, HEAD_DIM // 2, 1))
    return t * cos + rot * sin_signed


def _proj_kernel(x_ref, gpre_ref, w_ref, gqk_ref, bd_ref,
                 cos_a_ref, sin_a_ref, cos_bq_ref, sin_bq_ref, cos_bk_ref, sin_bk_ref,
                 qa_ref, ka_ref, vat_ref, qb_ref, kb_ref, vbt_ref):
    x = x_ref[0]
    h = _rms(x, gpre_ref[...]).astype(jnp.bfloat16)
    proj = jnp.dot(h, w_ref[...], preferred_element_type=jnp.float32)
    tm = x.shape[0]
    n_qk = (WIDTH + KV_WIDTH) // LANES

    cos_a = cos_a_ref[...]
    sin_a = sin_a_ref[...]
    for j in range(n_qk):
        t = proj[:, j * LANES:(j + 1) * LANES]
        ssq = jnp.dot((t * t).astype(jnp.bfloat16), bd_ref[...], preferred_element_type=jnp.float32)
        y = t * lax.rsqrt(ssq * (1.0 / HEAD_DIM) + EPS) * gqk_ref[:, j * LANES:(j + 1) * LANES]
        r = _rope_block(y, cos_a, sin_a).astype(jnp.bfloat16)
        if j < n_qk - 1:
            qa_ref[0, :, j * LANES:(j + 1) * LANES] = r
        else:
            ka_ref[0] = r

    off = WIDTH + KV_WIDTH
    for j in range(n_qk):
        t = proj[:, off + j * LANES: off + (j + 1) * LANES]
        if j < n_qk - 1:
            qb_ref[0, :, j * LANES:(j + 1) * LANES] = _rope_block(t, cos_bq_ref[...], sin_bq_ref[...]).astype(jnp.bfloat16)
        else:
            kb_ref[0] = _rope_block(t, cos_bk_ref[...], sin_bk_ref[...]).astype(jnp.bfloat16)

    ones = jnp.ones((ONES_ROWS, tm), jnp.bfloat16)
    for idx, vt_ref in enumerate((vat_ref, vbt_ref)):
        o = 2 * off + idx * KV_WIDTH
        vt = proj[:, o:o + KV_WIDTH].T.astype(jnp.bfloat16)
        for kvh in range(N_KV):
            base = kvh * VT_ROWS
            vt_ref[0, 0, base:base + HEAD_DIM, :] = vt[kvh * HEAD_DIM:(kvh + 1) * HEAD_DIM]
            vt_ref[0, 0, base + HEAD_DIM:base + VT_ROWS, :] = ones


def _masked_queries(q_ref, kvh):
    tq = q_ref.shape[1]
    lane = lax.broadcasted_iota(jnp.int32, (tq, LANES), 1)
    keep = (lane < HEAD_DIM) if kvh == 0 else (lane >= HEAD_DIM)
    blocks = [jnp.where(keep, q_ref[0, :, j * LANES:(j + 1) * LANES], jnp.zeros((), jnp.bfloat16))
              for j in range(GROUP)]
    return jnp.concatenate(blocks, axis=0)


def _group_norm_store(head_outs, gain_ref, o_ref):
    full = jnp.concatenate(head_outs, axis=0)
    ms = jnp.mean(full * full, axis=0, keepdims=True)
    o_ref[0] = (full * lax.rsqrt(ms + EPS) * gain_ref[...]).astype(o_ref.dtype)


_NT = (((1,), (1,)), ((), ()))


def _attn_global_kernel(q_ref, k_ref, vt_ref, gain_ref, o_ref, s_ref):
    n_chunks = k_ref.shape[1] // KEY_CHUNK
    tq = q_ref.shape[1]
    head_outs = [None] * N_HEADS
    for kvh in range(N_KV):
        qx = _masked_queries(q_ref, kvh)

        def scores(c, m):
            start = pl.multiple_of(c * KEY_CHUNK, KEY_CHUNK)
            s = lax.dot_general(k_ref[0, pl.ds(start, KEY_CHUNK), :], qx, _NT,
                                preferred_element_type=jnp.float32)
            s_ref[pl.ds(start, KEY_CHUNK), :] = s
            return jnp.maximum(m, jnp.max(s, axis=0, keepdims=True))

        m = lax.fori_loop(0, n_chunks, scores, jnp.full((1, GROUP * tq), -jnp.inf, jnp.float32))

        def weighted(c, acc):
            start = pl.multiple_of(c * KEY_CHUNK, KEY_CHUNK)
            p = jnp.exp2(s_ref[pl.ds(start, KEY_CHUNK), :] - m).astype(jnp.bfloat16)
            vt = vt_ref[0, c, kvh * VT_ROWS:(kvh + 1) * VT_ROWS, :]
            return acc + jnp.dot(vt, p, preferred_element_type=jnp.float32)

        acc = lax.fori_loop(0, n_chunks, weighted, jnp.zeros((VT_ROWS, GROUP * tq), jnp.float32))
        out = acc[:HEAD_DIM] / acc[HEAD_DIM:HEAD_DIM + 1]
        for j in range(GROUP):
            head_outs[kvh * GROUP + j] = out[:, j * tq:(j + 1) * tq]
    _group_norm_store(head_outs, gain_ref, o_ref)


def _attn_window_kernel(q_ref, k0_ref, k1_ref, k2_ref, v0_ref, v1_ref, v2_ref, sink_ref, gain_ref, o_ref,
                        *, seq_len):
    tq = q_ref.shape[1]
    n = pl.program_id(1)
    k_band = jnp.concatenate([k0_ref[0], k1_ref[0], k2_ref[0]], axis=0)
    rows = lax.broadcasted_iota(jnp.int32, (3 * tq, GROUP * tq), 0)
    cols = lax.broadcasted_iota(jnp.int32, (3 * tq, GROUP * tq), 1)
    kpos = (n - 1) * tq + rows
    qpos = n * tq + cols % tq
    valid = (jnp.abs(kpos - qpos) <= WINDOW) & (kpos >= 0) & (kpos < seq_len)
    head_outs = [None] * N_HEADS
    for kvh in range(N_KV):
        qx = _masked_queries(q_ref, kvh)
        s = lax.dot_general(k_band, qx, _NT, preferred_element_type=jnp.float32)
        s = jnp.where(valid, s, -jnp.inf)
        sink = sink_ref[kvh:kvh + 1, :]
        m = jnp.maximum(jnp.max(s, axis=0, keepdims=True), sink)
        p = jnp.exp2(s - m).astype(jnp.bfloat16)
        rs = slice(kvh * VT_ROWS, (kvh + 1) * VT_ROWS)
        vt = jnp.concatenate([v0_ref[0, 0, rs, :], v1_ref[0, 0, rs, :], v2_ref[0, 0, rs, :]], axis=1)
        acc = jnp.dot(vt, p, preferred_element_type=jnp.float32)
        den = acc[HEAD_DIM:HEAD_DIM + 1] + jnp.exp2(sink - m)
        out = acc[:HEAD_DIM] / den
        for j in range(GROUP):
            head_outs[kvh * GROUP + j] = out[:, j * tq:(j + 1) * tq]
    _group_norm_store(head_outs, gain_ref, o_ref)


_TN = (((0,), (0,)), ((), ()))


def _post_kernel(x_ref, at_ref, bt_ref, wo_ref, gpost_ref, gffn_ref, wg_ref, wu_ref, wd_ref, gfpost_ref,
                 o_ref, acc_ref):
    mixed = lax.dot_general(at_ref[0], wo_ref[:WIDTH, :], _TN, preferred_element_type=jnp.float32)
    mixed = mixed + lax.dot_general(bt_ref[0], wo_ref[WIDTH:, :], _TN, preferred_element_type=jnp.float32)
    x1 = x_ref[...] + _rms(mixed, gpost_ref[...])
    h = _rms(x1, gffn_ref[...]).astype(jnp.bfloat16)
    acc_ref[...] = jnp.zeros_like(acc_ref)

    def ffn_chunk(c, carry):
        g = jnp.dot(h, wg_ref[c], preferred_element_type=jnp.float32)
        u = jnp.dot(h, wu_ref[c], preferred_element_type=jnp.float32)
        a = (g / (1.0 + jnp.exp(-g)) * u).astype(jnp.bfloat16)
        start = pl.multiple_of(c * FF_CHUNK, FF_CHUNK)
        acc_ref[...] += jnp.dot(a, wd_ref[pl.ds(start, FF_CHUNK), :], preferred_element_type=jnp.float32)
        return carry

    lax.fori_loop(0, wg_ref.shape[0], ffn_chunk, 0)
    o_ref[...] = x1 + _rms(acc_ref[...], gfpost_ref[...])


def _rope_tables(seq_len):
    rows = seq_len // GRID_W
    t = np.arange(seq_len, dtype=np.float32)
    row = np.repeat(np.arange(rows, dtype=np.float32), GRID_W)
    col = np.tile(np.arange(GRID_W, dtype=np.float32), rows)
    ax_pairs = HEAD_DIM // 4
    freq_ax = jnp.asarray(ROPE_THETA, jnp.float32) ** (-jnp.arange(ax_pairs, dtype=jnp.float32) / ax_pairs)
    ang_axial = jnp.concatenate([row[:, None] * freq_ax[None, :], col[:, None] * freq_ax[None, :]], axis=-1)
    n_pairs = HEAD_DIM // 2
    freq_1d = jnp.asarray(ROPE_THETA, jnp.float32) ** (-jnp.arange(n_pairs, dtype=jnp.float32) / n_pairs)
    ang_1d = t[:, None] * freq_1d[None, :]

    def tables(ang, scale):
        cos = jnp.cos(ang) * scale
        sin = jnp.sin(ang) * scale
        cos_head = jnp.concatenate([cos, cos], axis=-1)
        sin_head = jnp.concatenate([-sin, sin], axis=-1)
        return jnp.tile(cos_head, (1, LANES // HEAD_DIM)), jnp.tile(sin_head, (1, LANES // HEAD_DIM))

    return tables(ang_axial, 1.0), tables(ang_1d, SCORE_SCALE), tables(ang_1d, 1.0)


def _const_spec(shape):
    return pl.BlockSpec(shape, lambda *_: (0,) * len(shape), pipeline_mode=pl.Buffered(1))


def kernel(x, norm_mix_pre, w_in, q_norm_a, k_norm_a, sink_b, group_norm_a, group_norm_b, w_out,
           norm_mix_post, norm_ffn_pre, w_gate, w_up, w_down, norm_ffn_post):
    B, S, D = x.shape
    assert D == D_MODEL and S % ROW_TILE == 0 and S % GRID_W == 0
    f32, bf16 = jnp.float32, jnp.bfloat16
    n_row_tiles = S // ROW_TILE
    n_q_tiles = S // Q_TILE

    w = w_in[0]
    q_cols = np.concatenate([np.arange(h * HEAD_DIM, (h + 1) * HEAD_DIM) for h in _Q_HEAD_ORDER])
    o_ka, o_va, o_qb, o_kb, o_vb = WIDTH, WIDTH + KV_WIDTH, WIDTH + 2 * KV_WIDTH, 2 * WIDTH + 2 * KV_WIDTH, 2 * WIDTH + 3 * KV_WIDTH
    col_idx = np.concatenate([q_cols, o_ka + np.arange(KV_WIDTH), o_qb + q_cols, o_kb + np.arange(KV_WIDTH),
                              o_va + np.arange(KV_WIDTH), o_vb + np.arange(KV_WIDTH)])
    w_p = w[:, col_idx].astype(bf16)
    gqk = jnp.concatenate([jnp.tile(q_norm_a[0] * SCORE_SCALE, N_HEADS), jnp.tile(k_norm_a[0], N_KV)])[None, :]
    head_of_lane = np.arange(LANES) // HEAD_DIM
    block_diag = jnp.asarray(head_of_lane[:, None] == head_of_lane[None, :], bf16)
    (cos_a, sin_a), (cos_bq, sin_bq), (cos_bk, sin_bk) = _rope_tables(S)

    table_spec = pl.BlockSpec((ROW_TILE, LANES), lambda i, b: (i, 0))
    q_out = pl.BlockSpec((1, ROW_TILE, WIDTH), lambda i, b: (b, i, 0))
    k_out = pl.BlockSpec((1, ROW_TILE, KV_WIDTH), lambda i, b: (b, i, 0))
    vt_out = pl.BlockSpec((1, 1, N_KV * VT_ROWS, ROW_TILE), lambda i, b: (b, i, 0, 0))
    qa, ka, vat, qb, kb, vbt = pl.pallas_call(
        _proj_kernel,
        grid=(n_row_tiles, B),
        in_specs=[pl.BlockSpec((1, ROW_TILE, D), lambda i, b: (b, i, 0)),
                  _const_spec((1, D)), _const_spec(w_p.shape), _const_spec(gqk.shape), _const_spec((LANES, LANES)),
                  table_spec, table_spec, table_spec, table_spec, table_spec, table_spec],
        out_specs=[q_out, k_out, vt_out, q_out, k_out, vt_out],
        out_shape=[jax.ShapeDtypeStruct((B, S, WIDTH), bf16), jax.ShapeDtypeStruct((B, S, KV_WIDTH), bf16),
                   jax.ShapeDtypeStruct((B, n_row_tiles, N_KV * VT_ROWS, ROW_TILE), bf16),
                   jax.ShapeDtypeStruct((B, S, WIDTH), bf16), jax.ShapeDtypeStruct((B, S, KV_WIDTH), bf16),
                   jax.ShapeDtypeStruct((B, n_row_tiles, N_KV * VT_ROWS, ROW_TILE), bf16)],
        compiler_params=pltpu.CompilerParams(dimension_semantics=("arbitrary", "arbitrary"),
                                             vmem_limit_bytes=VMEM_LIMIT),
        name="proj_rope",
    )(x, norm_mix_pre, w_p, gqk, block_diag, cos_a, sin_a, cos_bq, sin_bq, cos_bk, sin_bk)

    gain_a = jnp.broadcast_to(group_norm_a[0][:, None], (WIDTH, Q_TILE))
    out_t_spec = pl.BlockSpec((1, WIDTH, Q_TILE), lambda b, t: (b, 0, t))
    q_spec = pl.BlockSpec((1, Q_TILE, WIDTH), lambda b, t: (b, t, 0))
    a_t = pl.pallas_call(
        _attn_global_kernel,
        grid=(B, n_q_tiles),
        in_specs=[q_spec,
                  pl.BlockSpec((1, S, KV_WIDTH), lambda b, t: (b, 0, 0)),
                  pl.BlockSpec((1, n_row_tiles, N_KV * VT_ROWS, ROW_TILE), lambda b, t: (b, 0, 0, 0)),
                  _const_spec((WIDTH, Q_TILE))],
        out_specs=out_t_spec,
        out_shape=jax.ShapeDtypeStruct((B, WIDTH, S), bf16),
        scratch_shapes=[pltpu.VMEM((S, GROUP * Q_TILE), f32)],
        compiler_params=pltpu.CompilerParams(dimension_semantics=("arbitrary", "arbitrary"),
                                             vmem_limit_bytes=VMEM_LIMIT),
        name="attn_global",
    )(qa, ka, vat, gain_a)

    gain_b = jnp.broadcast_to(group_norm_b[0][:, None], (WIDTH, Q_TILE))
    sink_cols = jnp.repeat(sink_b[0] * LOG2E, Q_TILE).reshape(N_KV, GROUP * Q_TILE)
    sub = ROW_TILE // Q_TILE

    def k_band_spec(d):
        return pl.BlockSpec((1, Q_TILE, KV_WIDTH), lambda b, t: (b, jnp.clip(t + d, 0, n_q_tiles - 1), 0))

    def v_band_spec(d):
        def index(b, t):
            tt = jnp.clip(t + d, 0, n_q_tiles - 1)
            return (b, tt // sub, 0, tt % sub)
        return pl.BlockSpec((1, 1, N_KV * VT_ROWS, Q_TILE), index)

    b_t = pl.pallas_call(
        functools.partial(_attn_window_kernel, seq_len=S),
        grid=(B, n_q_tiles),
        in_specs=[q_spec, k_band_spec(-1), k_band_spec(0), k_band_spec(1),
                  v_band_spec(-1), v_band_spec(0), v_band_spec(1),
                  _const_spec(sink_cols.shape), _const_spec((WIDTH, Q_TILE))],
        out_specs=out_t_spec,
        out_shape=jax.ShapeDtypeStruct((B, WIDTH, S), bf16),
        compiler_params=pltpu.CompilerParams(dimension_semantics=("arbitrary", "arbitrary"),
                                             vmem_limit_bytes=VMEM_LIMIT),
        name="attn_window",
    )(qb, kb, kb, kb, vbt, vbt, vbt, sink_cols, gain_b)

    n_ff = D_FF // FF_CHUNK
    wg = w_gate[0].astype(bf16).reshape(D, n_ff, FF_CHUNK).transpose(1, 0, 2)
    wu = w_up[0].astype(bf16).reshape(D, n_ff, FF_CHUNK).transpose(1, 0, 2)
    wd = w_down[0].astype(bf16)
    wo = w_out[0].astype(bf16)
    x2 = x.reshape(B * S, D)
    row_spec = pl.BlockSpec((ROW_TILE, D), lambda i: (i, 0))
    t_spec = pl.BlockSpec((1, WIDTH, ROW_TILE), lambda i: (i // n_row_tiles, 0, i % n_row_tiles))
    out = pl.pallas_call(
        _post_kernel,
        grid=(B * n_row_tiles,),
        in_specs=[row_spec, t_spec, t_spec, _const_spec(wo.shape), _const_spec((1, D)), _const_spec((1, D)),
                  _const_spec(wg.shape), _const_spec(wu.shape), _const_spec(wd.shape), _const_spec((1, D))],
        out_specs=row_spec,
        out_shape=jax.ShapeDtypeStruct((B * S, D), f32),
        scratch_shapes=[pltpu.VMEM((ROW_TILE, D), f32)],
        compiler_params=pltpu.CompilerParams(dimension_semantics=("arbitrary",), vmem_limit_bytes=VMEM_LIMIT),
        name="outproj_ffn",
    )(x2, a_t, b_t, wo, norm_mix_post, norm_ffn_pre, wg, wu, wd, norm_ffn_post)
    return out.reshape(B, S, D)
```

```python
import math

import numpy as np
import jax
import jax.numpy as jnp
from jax import lax
from jax.experimental import pallas as pl
from jax.experimental.pallas import tpu as pltpu

D_MODEL = 1024
HEAD_DIM = 64
N_HEADS = 8
N_KV = 2
GROUP = N_HEADS // N_KV
WIDTH = N_HEADS * HEAD_DIM
KV_WIDTH = N_KV * HEAD_DIM
WINDOW = 128
GRID_W = 64
ROPE_THETA = 10000.0
EPS = 1e-6
D_FF = 2816
LOG2E = math.log2(math.e)
SCORE_SCALE = HEAD_DIM ** -0.5 * LOG2E

LANES = 128
ONES_ROWS = 16
VT_ROWS = HEAD_DIM + ONES_ROWS
ROW_TILE = 512
Q_TILE = 128
KEY_CHUNK = 512
WINDOW_SLOTS = (0, 2, 5, 9)
FF_CHUNK = 256
VMEM_LIMIT = 56 * 1024 * 1024

_Q_HEAD_ORDER = tuple(h for j in range(GROUP) for h in (j, j + GROUP))


def _rms(x, gain):
    return x * lax.rsqrt(jnp.mean(x * x, axis=-1, keepdims=True) + EPS) * gain


def _rope_block(t, cos, sin_signed):
    lane = lax.broadcasted_iota(jnp.int32, t.shape, 1)
    first_half = (lane % HEAD_DIM) < (HEAD_DIM // 2)
    rot = jnp.where(first_half, pltpu.roll(t, LANES - HEAD_DIM // 2, 1), pltpu.roll(t, HEAD_DIM // 2, 1))
    return t * cos + rot * sin_signed


def _proj_kernel(x_ref, gpre_ref, w_ref, gqk_ref, bd_ref,
                 cos_a_ref, sin_a_ref, cos_bq_ref, sin_bq_ref, cos_bk_ref, sin_bk_ref,
                 qa_ref, ka_ref, vat_ref, qb_ref, kb_ref, vbt_ref):
    x = x_ref[0]
    h = _rms(x, gpre_ref[...]).astype(jnp.bfloat16)
    proj = jnp.dot(h, w_ref[...], preferred_element_type=jnp.float32)
    tm = x.shape[0]
    n_qk = (WIDTH + KV_WIDTH) // LANES

    cos_a = cos_a_ref[...]
    sin_a = sin_a_ref[...]
    for j in range(n_qk):
        t = proj[:, j * LANES:(j + 1) * LANES]
        ssq = jnp.dot((t * t).astype(jnp.bfloat16), bd_ref[...], preferred_element_type=jnp.float32)
        y = t * lax.rsqrt(ssq * (1.0 / HEAD_DIM) + EPS) * gqk_ref[:, j * LANES:(j + 1) * LANES]
        r = _rope_block(y, cos_a, sin_a).astype(jnp.bfloat16)
        if j < n_qk - 1:
            qa_ref[0, :, j * LANES:(j + 1) * LANES] = r
        else:
            ka_ref[0] = r

    off = WIDTH + KV_WIDTH
    for j in range(n_qk):
        t = proj[:, off + j * LANES: off + (j + 1) * LANES]
        if j < n_qk - 1:
            qb_ref[0, :, j * LANES:(j + 1) * LANES] = _rope_block(t, cos_bq_ref[...], sin_bq_ref[...]).astype(jnp.bfloat16)
        else:
            kb_ref[0] = _rope_block(t, cos_bk_ref[...], sin_bk_ref[...]).astype(jnp.bfloat16)

    ones = jnp.ones((ONES_ROWS, tm), jnp.bfloat16)
    for idx, vt_ref in enumerate((vat_ref, vbt_ref)):
        o = 2 * off + idx * KV_WIDTH
        vt = proj[:, o:o + KV_WIDTH].T.astype(jnp.bfloat16)
        for kvh in range(N_KV):
            base = kvh * VT_ROWS
            vt_ref[0, 0, base:base + HEAD_DIM, :] = vt[kvh * HEAD_DIM:(kvh + 1) * HEAD_DIM]
            vt_ref[0, 0, base + HEAD_DIM:base + VT_ROWS, :] = ones


def _masked_queries(q_ref, kvh):
    tq = q_ref.shape[1]
    lane = lax.broadcasted_iota(jnp.int32, (tq, LANES), 1)
    keep = (lane < HEAD_DIM) if kvh == 0 else (lane >= HEAD_DIM)
    blocks = [jnp.where(keep, q_ref[0, :, j * LANES:(j + 1) * LANES], jnp.zeros((), jnp.bfloat16))
              for j in range(GROUP)]
    return jnp.concatenate(blocks, axis=0)


def _group_norm_store(head_outs, gain_ref, o_ref):
    full = jnp.concatenate(head_outs, axis=0)
    ms = jnp.mean(full * full, axis=0, keepdims=True)
    o_ref[0] = (full * lax.rsqrt(ms + EPS) * gain_ref[...]).astype(o_ref.dtype)


_NT = (((1,), (1,)), ((), ()))


def _attn_kernel(q_ref, qn_ref, k_ref, kn_ref, vt_ref, gain_a_ref,
                 qb_ref, kb0_ref, kb1_ref, kb2_ref, vb0_ref, vb1_ref, vb2_ref, bias_lo_ref, bias_hi_ref,
                 sink_ref, gain_b_ref, oa_ref, ob_ref, s_ref, m_ref, sw_ref):

    n_chunks = k_ref.shape[1] // KEY_CHUNK
    tq = q_ref.shape[1]
    n_cols = GROUP * tq
    neg_inf = jnp.full((1, n_cols), -jnp.inf, jnp.float32)

    def scores(c, m, qx, keys_ref, buf):
        start = pl.multiple_of(c * KEY_CHUNK, KEY_CHUNK)
        s = lax.dot_general(keys_ref[0, pl.ds(start, KEY_CHUNK), :], qx, _NT,
                            preferred_element_type=jnp.float32)
        s_ref[buf, pl.ds(start, KEY_CHUNK), :] = s
        return jnp.maximum(m, jnp.max(s, axis=0, keepdims=True))

    def weighted(c, acc, m, buf):
        start = pl.multiple_of(c * KEY_CHUNK, KEY_CHUNK)
        p = jnp.exp2(s_ref[buf, pl.ds(start, KEY_CHUNK), :] - m).astype(jnp.bfloat16)
        vt = vt_ref[0, c, buf * VT_ROWS:(buf + 1) * VT_ROWS, :]
        return acc + jnp.dot(vt, p, preferred_element_type=jnp.float32)

    @pl.when((pl.program_id(0) == 0) & (pl.program_id(1) == 0))
    def _():
        qx = _masked_queries(q_ref, 0)
        m_ref[0] = lax.fori_loop(0, n_chunks, lambda c, m: scores(c, m, qx, k_ref, 0), neg_inf)

    window_plan = {}
    for slot, event in zip(WINDOW_SLOTS, (("scores", 0), ("scores", 1), ("weighted", 0), ("weighted", 1))):
        window_plan.setdefault(min(slot, N_KV * n_chunks - 1), []).append(event)
    window_max = [None] * N_KV
    window_outs = [None] * N_HEADS

    head_outs = [None] * N_HEADS
    stages = ((1, q_ref, k_ref), (0, qn_ref, kn_ref))
    for kvh, (nxt, nq_ref, nk_ref) in enumerate(stages):
        qx = _masked_queries(nq_ref, nxt)
        m = m_ref[kvh]
        m_next = neg_inf
        acc = jnp.zeros((VT_ROWS, n_cols), jnp.float32)
        for c in range(n_chunks):
            m_next = scores(c, m_next, qx, nk_ref, nxt)
            acc = weighted(c, acc, m, kvh)
            for kind, wk in window_plan.get(kvh * n_chunks + c, ()):
                if kind == "scores":
                    window_max[wk] = _window_scores(qb_ref, (kb0_ref, kb1_ref, kb2_ref), bias_lo_ref, bias_hi_ref,
                                                    sink_ref, wk, sw_ref)
                else:
                    window_outs[wk * GROUP:(wk + 1) * GROUP] = _window_weighted(
                        (vb0_ref, vb1_ref, vb2_ref), sink_ref, wk, sw_ref, window_max[wk])
        m_ref[nxt] = m_next
        out = acc[:HEAD_DIM] / acc[HEAD_DIM:HEAD_DIM + 1]
        head_outs[kvh * GROUP:(kvh + 1) * GROUP] = [out[:, j * tq:(j + 1) * tq] for j in range(GROUP)]
    _group_norm_store(window_outs, gain_b_ref, ob_ref)
    _group_norm_store(head_outs, gain_a_ref, oa_ref)


def _window_scores(q_ref, k_refs, bias_lo_ref, bias_hi_ref, sink_ref, kvh, sw_ref):
    tq = q_ref.shape[1]
    qx = _masked_queries(q_ref, kvh)
    biases = (bias_lo_ref, None, bias_hi_ref)
    m = sink_ref[kvh:kvh + 1, :]
    for i, (k_ref, bias_ref) in enumerate(zip(k_refs, biases)):
        s = lax.dot_general(k_ref[0], qx, _NT, preferred_element_type=jnp.float32)
        if bias_ref is not None:
            s = s + bias_ref[0]
        sw_ref[kvh, i * tq:(i + 1) * tq, :] = s
        m = jnp.maximum(m, jnp.max(s, axis=0, keepdims=True))
    return m


def _window_weighted(v_refs, sink_ref, kvh, sw_ref, m):
    tq = v_refs[0].shape[3]
    p = jnp.exp2(sw_ref[kvh] - m).astype(jnp.bfloat16)
    rs = slice(kvh * VT_ROWS, (kvh + 1) * VT_ROWS)
    vt = jnp.concatenate([v_ref[0, 0, rs, :] for v_ref in v_refs], axis=1)
    acc = jnp.dot(vt, p, preferred_element_type=jnp.float32)
    den = acc[HEAD_DIM:HEAD_DIM + 1] + jnp.exp2(sink_ref[kvh:kvh + 1, :] - m)
    out = acc[:HEAD_DIM] / den
    return [out[:, j * tq:(j + 1) * tq] for j in range(GROUP)]


_TN = (((0,), (0,)), ((), ()))


def _post_kernel(x_ref, at_ref, bt_ref, wo_ref, gpost_ref, gffn_ref, wg_ref, wu_ref, wd_ref, gfpost_ref,
                 o_ref, acc_ref):
    mixed = lax.dot_general(at_ref[0], wo_ref[:WIDTH, :], _TN, preferred_element_type=jnp.float32)
    mixed = mixed + lax.dot_general(bt_ref[0], wo_ref[WIDTH:, :], _TN, preferred_element_type=jnp.float32)
    x1 = x_ref[...] + _rms(mixed, gpost_ref[...])
    h = _rms(x1, gffn_ref[...]).astype(jnp.bfloat16)
    acc_ref[...] = jnp.zeros_like(acc_ref)

    def ffn_chunk(c, carry):
        g = jnp.dot(h, wg_ref[c], preferred_element_type=jnp.float32)
        u = jnp.dot(h, wu_ref[c], preferred_element_type=jnp.float32)
        a = (g / (1.0 + jnp.exp(-g)) * u).astype(jnp.bfloat16)
        start = pl.multiple_of(c * FF_CHUNK, FF_CHUNK)
        acc_ref[...] += jnp.dot(a, wd_ref[pl.ds(start, FF_CHUNK), :], preferred_element_type=jnp.float32)
        return carry

    lax.fori_loop(0, wg_ref.shape[0], ffn_chunk, 0, unroll=True)
    o_ref[...] = x1 + _rms(acc_ref[...], gfpost_ref[...])


def _rope_tables(seq_len):
    rows = seq_len // GRID_W
    t = np.arange(seq_len, dtype=np.float32)
    row = np.repeat(np.arange(rows, dtype=np.float32), GRID_W)
    col = np.tile(np.arange(GRID_W, dtype=np.float32), rows)
    ax_pairs = HEAD_DIM // 4
    freq_ax = jnp.asarray(ROPE_THETA, jnp.float32) ** (-jnp.arange(ax_pairs, dtype=jnp.float32) / ax_pairs)
    ang_axial = jnp.concatenate([row[:, None] * freq_ax[None, :], col[:, None] * freq_ax[None, :]], axis=-1)
    n_pairs = HEAD_DIM // 2
    freq_1d = jnp.asarray(ROPE_THETA, jnp.float32) ** (-jnp.arange(n_pairs, dtype=jnp.float32) / n_pairs)
    ang_1d = t[:, None] * freq_1d[None, :]

    def tables(ang, scale):
        cos = jnp.cos(ang) * scale
        sin = jnp.sin(ang) * scale
        cos_head = jnp.concatenate([cos, cos], axis=-1)
        sin_head = jnp.concatenate([-sin, sin], axis=-1)
        return jnp.tile(cos_head, (1, LANES // HEAD_DIM)), jnp.tile(sin_head, (1, LANES // HEAD_DIM))

    return tables(ang_axial, 1.0), tables(ang_1d, SCORE_SCALE), tables(ang_1d, 1.0)


def _const_spec(shape):
    return pl.BlockSpec(shape, lambda *_: (0,) * len(shape), pipeline_mode=pl.Buffered(1))


def kernel(x, norm_mix_pre, w_in, q_norm_a, k_norm_a, sink_b, group_norm_a, group_norm_b, w_out,
           norm_mix_post, norm_ffn_pre, w_gate, w_up, w_down, norm_ffn_post):
    B, S, D = x.shape
    assert D == D_MODEL and S % ROW_TILE == 0 and S % GRID_W == 0
    f32, bf16 = jnp.float32, jnp.bfloat16
    n_row_tiles = S // ROW_TILE
    n_q_tiles = S // Q_TILE

    w = w_in[0]
    q_cols = np.concatenate([np.arange(h * HEAD_DIM, (h + 1) * HEAD_DIM) for h in _Q_HEAD_ORDER])
    o_ka, o_va, o_qb, o_kb, o_vb = WIDTH, WIDTH + KV_WIDTH, WIDTH + 2 * KV_WIDTH, 2 * WIDTH + 2 * KV_WIDTH, 2 * WIDTH + 3 * KV_WIDTH
    col_idx = np.concatenate([q_cols, o_ka + np.arange(KV_WIDTH), o_qb + q_cols, o_kb + np.arange(KV_WIDTH),
                              o_va + np.arange(KV_WIDTH), o_vb + np.arange(KV_WIDTH)])
    w_p = w[:, col_idx].astype(bf16)
    gqk = jnp.concatenate([jnp.tile(q_norm_a[0] * SCORE_SCALE, N_HEADS), jnp.tile(k_norm_a[0], N_KV)])[None, :]
    head_of_lane = np.arange(LANES) // HEAD_DIM
    block_diag = jnp.asarray(head_of_lane[:, None] == head_of_lane[None, :], bf16)
    (cos_a, sin_a), (cos_bq, sin_bq), (cos_bk, sin_bk) = _rope_tables(S)

    table_spec = pl.BlockSpec((ROW_TILE, LANES), lambda i, b: (i, 0))
    q_out = pl.BlockSpec((1, ROW_TILE, WIDTH), lambda i, b: (b, i, 0))
    k_out = pl.BlockSpec((1, ROW_TILE, KV_WIDTH), lambda i, b: (b, i, 0))
    vt_out = pl.BlockSpec((1, 1, N_KV * VT_ROWS, ROW_TILE), lambda i, b: (b, i, 0, 0))
    qa, ka, vat, qb, kb, vbt = pl.pallas_call(
        _proj_kernel,
        grid=(n_row_tiles, B),
        in_specs=[pl.BlockSpec((1, ROW_TILE, D), lambda i, b: (b, i, 0)),
                  _const_spec((1, D)), _const_spec(w_p.shape), _const_spec(gqk.shape), _const_spec((LANES, LANES)),
                  table_spec, table_spec, table_spec, table_spec, table_spec, table_spec],
        out_specs=[q_out, k_out, vt_out, q_out, k_out, vt_out],
        out_shape=[jax.ShapeDtypeStruct((B, S, WIDTH), bf16), jax.ShapeDtypeStruct((B, S, KV_WIDTH), bf16),
                   jax.ShapeDtypeStruct((B, n_row_tiles, N_KV * VT_ROWS, ROW_TILE), bf16),
                   jax.ShapeDtypeStruct((B, S, WIDTH), bf16), jax.ShapeDtypeStruct((B, S, KV_WIDTH), bf16),
                   jax.ShapeDtypeStruct((B, n_row_tiles, N_KV * VT_ROWS, ROW_TILE), bf16)],
        compiler_params=pltpu.CompilerParams(dimension_semantics=("arbitrary", "arbitrary"),
                                             vmem_limit_bytes=VMEM_LIMIT),
        name="proj_rope",
    )(x, norm_mix_pre, w_p, gqk, block_diag, cos_a, sin_a, cos_bq, sin_bq, cos_bk, sin_bk)

    gain_a = jnp.broadcast_to(group_norm_a[0][:, None], (WIDTH, Q_TILE))
    out_t_spec = pl.BlockSpec((1, WIDTH, Q_TILE), lambda b, t: (b, 0, t))
    q_spec = pl.BlockSpec((1, Q_TILE, WIDTH), lambda b, t: (b, t, 0))
    last_tile = B * n_q_tiles - 1

    def next_tile(b, t):
        nxt = jnp.minimum(b * n_q_tiles + t + 1, last_tile)
        return nxt // n_q_tiles, nxt % n_q_tiles

    gain_b = jnp.broadcast_to(group_norm_b[0][:, None], (WIDTH, Q_TILE))
    sink_cols = jnp.repeat(sink_b[0] * LOG2E, Q_TILE).reshape(N_KV, GROUP * Q_TILE)
    sub = ROW_TILE // Q_TILE
    key_row = np.arange(Q_TILE)[:, None]
    q_col = np.arange(GROUP * Q_TILE)[None, :] % Q_TILE
    neg = np.float32(-np.inf)
    band_bias = jnp.asarray(np.stack([np.where(q_col <= key_row, np.float32(0), neg),
                                      np.where(key_row <= q_col, np.float32(0), neg),
                                      np.full((Q_TILE, GROUP * Q_TILE), neg)]).astype(np.float32))

    def k_band_spec(d):
        return pl.BlockSpec((1, Q_TILE, KV_WIDTH), lambda b, t: (b, jnp.clip(t + d, 0, n_q_tiles - 1), 0))

    def v_band_spec(d):
        def index(b, t):
            tt = jnp.clip(t + d, 0, n_q_tiles - 1)
            return (b, tt // sub, 0, tt % sub)
        return pl.BlockSpec((1, 1, N_KV * VT_ROWS, Q_TILE), index)

    bias_block = (1, Q_TILE, GROUP * Q_TILE)
    a_t, b_t = pl.pallas_call(
        _attn_kernel,
        grid=(B, n_q_tiles),
        in_specs=[q_spec,
                  pl.BlockSpec((1, Q_TILE, WIDTH), lambda b, t: (*next_tile(b, t), 0)),
                  pl.BlockSpec((1, S, KV_WIDTH), lambda b, t: (b, 0, 0)),
                  pl.BlockSpec((1, S, KV_WIDTH), lambda b, t: (next_tile(b, t)[0], 0, 0)),
                  pl.BlockSpec((1, n_row_tiles, N_KV * VT_ROWS, ROW_TILE), lambda b, t: (b, 0, 0, 0)),
                  _const_spec((WIDTH, Q_TILE)),
                  q_spec, k_band_spec(-1), k_band_spec(0), k_band_spec(1),
                  v_band_spec(-1), v_band_spec(0), v_band_spec(1),
                  pl.BlockSpec(bias_block, lambda b, t: (jnp.where(t == 0, 2, 0), 0, 0)),
                  pl.BlockSpec(bias_block, lambda b, t: (jnp.where(t == n_q_tiles - 1, 2, 1), 0, 0)),
                  _const_spec(sink_cols.shape), _const_spec((WIDTH, Q_TILE))],
        out_specs=[out_t_spec, out_t_spec],
        out_shape=[jax.ShapeDtypeStruct((B, WIDTH, S), bf16), jax.ShapeDtypeStruct((B, WIDTH, S), bf16)],
        scratch_shapes=[pltpu.VMEM((N_KV, S, GROUP * Q_TILE), f32),
                        pltpu.VMEM((N_KV, 1, GROUP * Q_TILE), f32),
                        pltpu.VMEM((N_KV, 3 * Q_TILE, GROUP * Q_TILE), f32)],
        compiler_params=pltpu.CompilerParams(dimension_semantics=("arbitrary", "arbitrary"),
                                             vmem_limit_bytes=VMEM_LIMIT),
        name="attention",
    )(qa, qa, ka, ka, vat, gain_a, qb, kb, kb, kb, vbt, vbt, vbt, band_bias, band_bias, sink_cols, gain_b)

    n_ff = D_FF // FF_CHUNK
    wg = w_gate[0].astype(bf16).reshape(D, n_ff, FF_CHUNK).transpose(1, 0, 2)
    wu = w_up[0].astype(bf16).reshape(D, n_ff, FF_CHUNK).transpose(1, 0, 2)
    wd = w_down[0].astype(bf16)
    wo = w_out[0].astype(bf16)
    x2 = x.reshape(B * S, D)
    row_spec = pl.BlockSpec((ROW_TILE, D), lambda i: (i, 0))
    t_spec = pl.BlockSpec((1, WIDTH, ROW_TILE), lambda i: (i // n_row_tiles, 0, i % n_row_tiles))
    out = pl.pallas_call(
        _post_kernel,
        grid=(B * n_row_tiles,),
        in_specs=[row_spec, t_spec, t_spec, _const_spec(wo.shape), _const_spec((1, D)), _const_spec((1, D)),
                  _const_spec(wg.shape), _const_spec(wu.shape), _const_spec(wd.shape), _const_spec((1, D))],
        out_specs=row_spec,
        out_shape=jax.ShapeDtypeStruct((B * S, D), f32),
        scratch_shapes=[pltpu.VMEM((ROW_TILE, D), f32)],
        compiler_params=pltpu.CompilerParams(dimension_semantics=("arbitrary",), vmem_limit_bytes=VMEM_LIMIT),
        name="outproj_ffn",
    )(x2, a_t, b_t, wo, norm_mix_post, norm_ffn_pre, wg, wu, wd, norm_ffn_post)
    return out.reshape(B, S, D)
```

```python
import math

import numpy as np
import jax
import jax.numpy as jnp
from jax import lax
from jax.experimental import pallas as pl
from jax.experimental.pallas import tpu as pltpu

D_MODEL = 1024
HEAD_DIM = 64
N_HEADS = 8
N_KV = 2
GROUP = N_HEADS // N_KV
WIDTH = N_HEADS * HEAD_DIM
KV_WIDTH = N_KV * HEAD_DIM
WINDOW = 128
GRID_W = 64
ROPE_THETA = 10000.0
EPS = 1e-6
D_FF = 2816
LOG2E = math.log2(math.e)
SCORE_SCALE = HEAD_DIM ** -0.5 * LOG2E

LANES = 128
ONES_ROWS = 16
VT_ROWS = HEAD_DIM + ONES_ROWS
ROW_TILE = 512
Q_TILE = 128
KEY_CHUNK = 512
WINDOW_SLOTS = (0, 2, 5, 9)
FF_CHUNK = 256
VMEM_LIMIT = 56 * 1024 * 1024

_Q_HEAD_ORDER = tuple(h for j in range(GROUP) for h in (j, j + GROUP))


def _rms(x, gain):
    return x * lax.rsqrt(jnp.mean(x * x, axis=-1, keepdims=True) + EPS) * gain


def _rope_block(t, cos, sin_signed):
    lane = lax.broadcasted_iota(jnp.int32, t.shape, 1)
    first_half = (lane % HEAD_DIM) < (HEAD_DIM // 2)
    rot = jnp.where(first_half, pltpu.roll(t, LANES - HEAD_DIM // 2, 1), pltpu.roll(t, HEAD_DIM // 2, 1))
    return t * cos + rot * sin_signed


def _proj_kernel(x_ref, gpre_ref, w_ref, gqk_ref, bd_ref,
                 cos_a_ref, sin_a_ref, cos_bq_ref, sin_bq_ref, cos_bk_ref, sin_bk_ref,
                 qa_ref, ka_ref, vat_ref, qb_ref, kb_ref, vbt_ref):
    x = x_ref[0]
    h = _rms(x, gpre_ref[...]).astype(jnp.bfloat16)
    proj = jnp.dot(h, w_ref[...], preferred_element_type=jnp.float32)
    tm = x.shape[0]
    n_qk = (WIDTH + KV_WIDTH) // LANES

    cos_a = cos_a_ref[...]
    sin_a = sin_a_ref[...]
    for j in range(n_qk):
        t = proj[:, j * LANES:(j + 1) * LANES]
        ssq = jnp.dot((t * t).astype(jnp.bfloat16), bd_ref[...], preferred_element_type=jnp.float32)
        y = t * lax.rsqrt(ssq * (1.0 / HEAD_DIM) + EPS) * gqk_ref[:, j * LANES:(j + 1) * LANES]
        r = _rope_block(y, cos_a, sin_a).astype(jnp.bfloat16)
        if j < n_qk - 1:
            qa_ref[0, :, j * LANES:(j + 1) * LANES] = r
        else:
            ka_ref[0] = r

    off = WIDTH + KV_WIDTH
    for j in range(n_qk):
        t = proj[:, off + j * LANES: off + (j + 1) * LANES]
        if j < n_qk - 1:
            qb_ref[0, :, j * LANES:(j + 1) * LANES] = _rope_block(t, cos_bq_ref[...], sin_bq_ref[...]).astype(jnp.bfloat16)
        else:
            kb_ref[0] = _rope_block(t, cos_bk_ref[...], sin_bk_ref[...]).astype(jnp.bfloat16)

    ones = jnp.ones((ONES_ROWS, tm), jnp.bfloat16)
    for idx, vt_ref in enumerate((vat_ref, vbt_ref)):
        o = 2 * off + idx * KV_WIDTH
        vt = proj[:, o:o + KV_WIDTH].T.astype(jnp.bfloat16)
        for kvh in range(N_KV):
            base = kvh * VT_ROWS
            vt_ref[0, 0, base:base + HEAD_DIM, :] = vt[kvh * HEAD_DIM:(kvh + 1) * HEAD_DIM]
            vt_ref[0, 0, base + HEAD_DIM:base + VT_ROWS, :] = ones


def _masked_queries(q_ref, kvh):
    tq = q_ref.shape[1]
    lane = lax.broadcasted_iota(jnp.int32, (tq, LANES), 1)
    keep = (lane < HEAD_DIM) if kvh == 0 else (lane >= HEAD_DIM)
    blocks = [jnp.where(keep, q_ref[0, :, j * LANES:(j + 1) * LANES], jnp.zeros((), jnp.bfloat16))
              for j in range(GROUP)]
    return jnp.concatenate(blocks, axis=0)


def _group_norm_store(head_outs, gain_ref, o_ref):
    full = jnp.concatenate(head_outs, axis=0)
    ms = jnp.mean(full * full, axis=0, keepdims=True)
    o_ref[0] = (full * lax.rsqrt(ms + EPS) * gain_ref[...]).astype(o_ref.dtype)


_NT = (((1,), (1,)), ((), ()))


def _attn_kernel(q_ref, qn_ref, k_ref, kn_ref, vt_ref, gain_a_ref,
                 qb_ref, kb0_ref, kb1_ref, kb2_ref, vb0_ref, vb1_ref, vb2_ref, bias_lo_ref, bias_hi_ref,
                 sink_ref, gain_b_ref, oa_ref, ob_ref, s_ref, m_ref, sw_ref):

    n_chunks = k_ref.shape[1] // KEY_CHUNK
    tq = q_ref.shape[1]
    n_cols = GROUP * tq
    neg_inf = jnp.full((1, n_cols), -jnp.inf, jnp.bfloat16)

    def scores(c, m, qx, keys_ref, buf):
        start = pl.multiple_of(c * KEY_CHUNK, KEY_CHUNK)
        s = lax.dot_general(keys_ref[0, pl.ds(start, KEY_CHUNK), :], qx, _NT,
                            preferred_element_type=jnp.float32)
        s = s.astype(jnp.bfloat16)
        s_ref[buf, pl.ds(start, KEY_CHUNK), :] = s
        return jnp.maximum(m, jnp.max(s, axis=0, keepdims=True))

    def weighted(c, acc, m, buf):
        start = pl.multiple_of(c * KEY_CHUNK, KEY_CHUNK)
        p = jnp.exp2(s_ref[buf, pl.ds(start, KEY_CHUNK), :] - m)
        vt = vt_ref[0, c, buf * VT_ROWS:(buf + 1) * VT_ROWS, :]
        return acc + jnp.dot(vt, p, preferred_element_type=jnp.float32)

    @pl.when((pl.program_id(0) == 0) & (pl.program_id(1) == 0))
    def _():
        qx = _masked_queries(q_ref, 0)
        m0 = lax.fori_loop(0, n_chunks, lambda c, m: scores(c, m, qx, k_ref, 0), neg_inf)
        m_ref[0] = m0.astype(jnp.float32)

    window_plan = {}
    for slot, event in zip(WINDOW_SLOTS, (("scores", 0), ("scores", 1), ("weighted", 0), ("weighted", 1))):
        window_plan.setdefault(min(slot, N_KV * n_chunks - 1), []).append(event)
    window_max = [None] * N_KV
    window_outs = [None] * N_HEADS

    head_outs = [None] * N_HEADS
    stages = ((1, q_ref, k_ref), (0, qn_ref, kn_ref))
    for kvh, (nxt, nq_ref, nk_ref) in enumerate(stages):
        qx = _masked_queries(nq_ref, nxt)
        m = m_ref[kvh].astype(jnp.bfloat16)
        m_next = neg_inf
        acc = jnp.zeros((VT_ROWS, n_cols), jnp.float32)
        for c in range(n_chunks):
            m_next = scores(c, m_next, qx, nk_ref, nxt)
            acc = weighted(c, acc, m, kvh)
            for kind, wk in window_plan.get(kvh * n_chunks + c, ()):
                if kind == "scores":
                    window_max[wk] = _window_scores(qb_ref, (kb0_ref, kb1_ref, kb2_ref), bias_lo_ref, bias_hi_ref,
                                                    sink_ref, wk, sw_ref)
                else:
                    window_outs[wk * GROUP:(wk + 1) * GROUP] = _window_weighted(
                        (vb0_ref, vb1_ref, vb2_ref), sink_ref, wk, sw_ref, window_max[wk])
        m_ref[nxt] = m_next.astype(jnp.float32)
        out = acc[:HEAD_DIM] / acc[HEAD_DIM:HEAD_DIM + 1]
        head_outs[kvh * GROUP:(kvh + 1) * GROUP] = [out[:, j * tq:(j + 1) * tq] for j in range(GROUP)]
    _group_norm_store(window_outs, gain_b_ref, ob_ref)
    _group_norm_store(head_outs, gain_a_ref, oa_ref)


def _window_scores(q_ref, k_refs, bias_lo_ref, bias_hi_ref, sink_ref, kvh, sw_ref):
    tq = q_ref.shape[1]
    qx = _masked_queries(q_ref, kvh)
    biases = (bias_lo_ref, None, bias_hi_ref)
    m = sink_ref[kvh:kvh + 1, :].astype(jnp.bfloat16)
    for i, (k_ref, bias_ref) in enumerate(zip(k_refs, biases)):
        s = lax.dot_general(k_ref[0], qx, _NT, preferred_element_type=jnp.float32)
        if bias_ref is not None:
            s = s + bias_ref[0]
        s = s.astype(jnp.bfloat16)
        sw_ref[kvh, i * tq:(i + 1) * tq, :] = s
        m = jnp.maximum(m, jnp.max(s, axis=0, keepdims=True))
    return m


def _window_weighted(v_refs, sink_ref, kvh, sw_ref, m):
    tq = v_refs[0].shape[3]
    p = jnp.exp2(sw_ref[kvh] - m)
    rs = slice(kvh * VT_ROWS, (kvh + 1) * VT_ROWS)
    vt = jnp.concatenate([v_ref[0, 0, rs, :] for v_ref in v_refs], axis=1)
    acc = jnp.dot(vt, p, preferred_element_type=jnp.float32)
    den = acc[HEAD_DIM:HEAD_DIM + 1] + jnp.exp2(sink_ref[kvh:kvh + 1, :] - m.astype(jnp.float32))
    out = acc[:HEAD_DIM] / den
    return [out[:, j * tq:(j + 1) * tq] for j in range(GROUP)]


_TN = (((0,), (0,)), ((), ()))


def _post_kernel(x_ref, at_ref, bt_ref, wo_ref, gpost_ref, gffn_ref, wg_ref, wu_ref, wd_ref, gfpost_ref,
                 o_ref, acc_ref):
    mixed = lax.dot_general(at_ref[0], wo_ref[:WIDTH, :], _TN, preferred_element_type=jnp.float32)
    mixed = mixed + lax.dot_general(bt_ref[0], wo_ref[WIDTH:, :], _TN, preferred_element_type=jnp.float32)
    x1 = x_ref[...] + _rms(mixed, gpost_ref[...])
    h = _rms(x1, gffn_ref[...]).astype(jnp.bfloat16)
    acc_ref[...] = jnp.zeros_like(acc_ref)

    def ffn_chunk(c, carry):
        g = jnp.dot(h, wg_ref[c], preferred_element_type=jnp.float32)
        u = jnp.dot(h, wu_ref[c], preferred_element_type=jnp.float32)
        a = (g / (1.0 + jnp.exp(-g)) * u).astype(jnp.bfloat16)
        start = pl.multiple_of(c * FF_CHUNK, FF_CHUNK)
        acc_ref[...] += jnp.dot(a, wd_ref[pl.ds(start, FF_CHUNK), :], preferred_element_type=jnp.float32)
        return carry

    lax.fori_loop(0, wg_ref.shape[0], ffn_chunk, 0, unroll=True)
    o_ref[...] = x1 + _rms(acc_ref[...], gfpost_ref[...])


def _rope_tables(seq_len):
    rows = seq_len // GRID_W
    t = np.arange(seq_len, dtype=np.float32)
    row = np.repeat(np.arange(rows, dtype=np.float32), GRID_W)
    col = np.tile(np.arange(GRID_W, dtype=np.float32), rows)
    ax_pairs = HEAD_DIM // 4
    freq_ax = jnp.asarray(ROPE_THETA, jnp.float32) ** (-jnp.arange(ax_pairs, dtype=jnp.float32) / ax_pairs)
    ang_axial = jnp.concatenate([row[:, None] * freq_ax[None, :], col[:, None] * freq_ax[None, :]], axis=-1)
    n_pairs = HEAD_DIM // 2
    freq_1d = jnp.asarray(ROPE_THETA, jnp.float32) ** (-jnp.arange(n_pairs, dtype=jnp.float32) / n_pairs)
    ang_1d = t[:, None] * freq_1d[None, :]

    def tables(ang, scale):
        cos = jnp.cos(ang) * scale
        sin = jnp.sin(ang) * scale
        cos_head = jnp.concatenate([cos, cos], axis=-1)
        sin_head = jnp.concatenate([-sin, sin], axis=-1)
        return jnp.tile(cos_head, (1, LANES // HEAD_DIM)), jnp.tile(sin_head, (1, LANES // HEAD_DIM))

    return tables(ang_axial, 1.0), tables(ang_1d, SCORE_SCALE), tables(ang_1d, 1.0)


def _const_spec(shape):
    return pl.BlockSpec(shape, lambda *_: (0,) * len(shape), pipeline_mode=pl.Buffered(1))


def kernel(x, norm_mix_pre, w_in, q_norm_a, k_norm_a, sink_b, group_norm_a, group_norm_b, w_out,
           norm_mix_post, norm_ffn_pre, w_gate, w_up, w_down, norm_ffn_post):
    B, S, D = x.shape
    assert D == D_MODEL and S % ROW_TILE == 0 and S % GRID_W == 0
    f32, bf16 = jnp.float32, jnp.bfloat16
    n_row_tiles = S // ROW_TILE
    n_q_tiles = S // Q_TILE

    w = w_in[0]
    q_cols = np.concatenate([np.arange(h * HEAD_DIM, (h + 1) * HEAD_DIM) for h in _Q_HEAD_ORDER])
    o_ka, o_va, o_qb, o_kb, o_vb = WIDTH, WIDTH + KV_WIDTH, WIDTH + 2 * KV_WIDTH, 2 * WIDTH + 2 * KV_WIDTH, 2 * WIDTH + 3 * KV_WIDTH
    col_idx = np.concatenate([q_cols, o_ka + np.arange(KV_WIDTH), o_qb + q_cols, o_kb + np.arange(KV_WIDTH),
                              o_va + np.arange(KV_WIDTH), o_vb + np.arange(KV_WIDTH)])
    w_p = w[:, col_idx].astype(bf16)
    gqk = jnp.concatenate([jnp.tile(q_norm_a[0] * SCORE_SCALE, N_HEADS), jnp.tile(k_norm_a[0], N_KV)])[None, :]
    head_of_lane = np.arange(LANES) // HEAD_DIM
    block_diag = jnp.asarray(head_of_lane[:, None] == head_of_lane[None, :], bf16)
    (cos_a, sin_a), (cos_bq, sin_bq), (cos_bk, sin_bk) = _rope_tables(S)

    table_spec = pl.BlockSpec((ROW_TILE, LANES), lambda i, b: (i, 0))
    q_out = pl.BlockSpec((1, ROW_TILE, WIDTH), lambda i, b: (b, i, 0))
    k_out = pl.BlockSpec((1, ROW_TILE, KV_WIDTH), lambda i, b: (b, i, 0))
    vt_out = pl.BlockSpec((1, 1, N_KV * VT_ROWS, ROW_TILE), lambda i, b: (b, i, 0, 0))
    qa, ka, vat, qb, kb, vbt = pl.pallas_call(
        _proj_kernel,
        grid=(n_row_tiles, B),
        in_specs=[pl.BlockSpec((1, ROW_TILE, D), lambda i, b: (b, i, 0)),
                  _const_spec((1, D)), _const_spec(w_p.shape), _const_spec(gqk.shape), _const_spec((LANES, LANES)),
                  table_spec, table_spec, table_spec, table_spec, table_spec, table_spec],
        out_specs=[q_out, k_out, vt_out, q_out, k_out, vt_out],
        out_shape=[jax.ShapeDtypeStruct((B, S, WIDTH), bf16), jax.ShapeDtypeStruct((B, S, KV_WIDTH), bf16),
                   jax.ShapeDtypeStruct((B, n_row_tiles, N_KV * VT_ROWS, ROW_TILE), bf16),
                   jax.ShapeDtypeStruct((B, S, WIDTH), bf16), jax.ShapeDtypeStruct((B, S, KV_WIDTH), bf16),
                   jax.ShapeDtypeStruct((B, n_row_tiles, N_KV * VT_ROWS, ROW_TILE), bf16)],
        compiler_params=pltpu.CompilerParams(dimension_semantics=("arbitrary", "arbitrary"),
                                             vmem_limit_bytes=VMEM_LIMIT),
        name="proj_rope",
    )(x, norm_mix_pre, w_p, gqk, block_diag, cos_a, sin_a, cos_bq, sin_bq, cos_bk, sin_bk)

    gain_a = jnp.broadcast_to(group_norm_a[0][:, None], (WIDTH, Q_TILE))
    out_t_spec = pl.BlockSpec((1, WIDTH, Q_TILE), lambda b, t: (b, 0, t))
    q_spec = pl.BlockSpec((1, Q_TILE, WIDTH), lambda b, t: (b, t, 0))
    last_tile = B * n_q_tiles - 1

    def next_tile(b, t):
        nxt = jnp.minimum(b * n_q_tiles + t + 1, last_tile)
        return nxt // n_q_tiles, nxt % n_q_tiles

    gain_b = jnp.broadcast_to(group_norm_b[0][:, None], (WIDTH, Q_TILE))
    sink_cols = jnp.repeat(sink_b[0] * LOG2E, Q_TILE).reshape(N_KV, GROUP * Q_TILE)
    sub = ROW_TILE // Q_TILE
    key_row = np.arange(Q_TILE)[:, None]
    q_col = np.arange(GROUP * Q_TILE)[None, :] % Q_TILE
    neg = np.float32(-np.inf)
    band_bias = jnp.asarray(np.stack([np.where(q_col <= key_row, np.float32(0), neg),
                                      np.where(key_row <= q_col, np.float32(0), neg),
                                      np.full((Q_TILE, GROUP * Q_TILE), neg)]).astype(np.float32))

    def k_band_spec(d):
        return pl.BlockSpec((1, Q_TILE, KV_WIDTH), lambda b, t: (b, jnp.clip(t + d, 0, n_q_tiles - 1), 0))

    def v_band_spec(d):
        def index(b, t):
            tt = jnp.clip(t + d, 0, n_q_tiles - 1)
            return (b, tt // sub, 0, tt % sub)
        return pl.BlockSpec((1, 1, N_KV * VT_ROWS, Q_TILE), index)

    bias_block = (1, Q_TILE, GROUP * Q_TILE)
    a_t, b_t = pl.pallas_call(
        _attn_kernel,
        grid=(B, n_q_tiles),
        in_specs=[q_spec,
                  pl.BlockSpec((1, Q_TILE, WIDTH), lambda b, t: (*next_tile(b, t), 0)),
                  pl.BlockSpec((1, S, KV_WIDTH), lambda b, t: (b, 0, 0)),
                  pl.BlockSpec((1, S, KV_WIDTH), lambda b, t: (next_tile(b, t)[0], 0, 0)),
                  pl.BlockSpec((1, n_row_tiles, N_KV * VT_ROWS, ROW_TILE), lambda b, t: (b, 0, 0, 0)),
                  _const_spec((WIDTH, Q_TILE)),
                  q_spec, k_band_spec(-1), k_band_spec(0), k_band_spec(1),
                  v_band_spec(-1), v_band_spec(0), v_band_spec(1),
                  pl.BlockSpec(bias_block, lambda b, t: (jnp.where(t == 0, 2, 0), 0, 0)),
                  pl.BlockSpec(bias_block, lambda b, t: (jnp.where(t == n_q_tiles - 1, 2, 1), 0, 0)),
                  _const_spec(sink_cols.shape), _const_spec((WIDTH, Q_TILE))],
        out_specs=[out_t_spec, out_t_spec],
        out_shape=[jax.ShapeDtypeStruct((B, WIDTH, S), bf16), jax.ShapeDtypeStruct((B, WIDTH, S), bf16)],
        scratch_shapes=[pltpu.VMEM((N_KV, S, GROUP * Q_TILE), bf16),
                        pltpu.VMEM((N_KV, 1, GROUP * Q_TILE), f32),
                        pltpu.VMEM((N_KV, 3 * Q_TILE, GROUP * Q_TILE), bf16)],
        compiler_params=pltpu.CompilerParams(dimension_semantics=("arbitrary", "arbitrary"),
                                             vmem_limit_bytes=VMEM_LIMIT),
        name="attention",
    )(qa, qa, ka, ka, vat, gain_a, qb, kb, kb, kb, vbt, vbt, vbt, band_bias, band_bias, sink_cols, gain_b)

    n_ff = D_FF // FF_CHUNK
    wg = w_gate[0].astype(bf16).reshape(D, n_ff, FF_CHUNK).transpose(1, 0, 2)
    wu = w_up[0].astype(bf16).reshape(D, n_ff, FF_CHUNK).transpose(1, 0, 2)
    wd = w_down[0].astype(bf16)
    wo = w_out[0].astype(bf16)
    x2 = x.reshape(B * S, D)
    row_spec = pl.BlockSpec((ROW_TILE, D), lambda i: (i, 0))
    t_spec = pl.BlockSpec((1, WIDTH, ROW_TILE), lambda i: (i // n_row_tiles, 0, i % n_row_tiles))
    out = pl.pallas_call(
        _post_kernel,
        grid=(B * n_row_tiles,),
        in_specs=[row_spec, t_spec, t_spec, _const_spec(wo.shape), _const_spec((1, D)), _const_spec((1, D)),
                  _const_spec(wg.shape), _const_spec(wu.shape), _const_spec(wd.shape), _const_spec((1, D))],
        out_specs=row_spec,
        out_shape=jax.ShapeDtypeStruct((B * S, D), f32),
        scratch_shapes=[pltpu.VMEM((ROW_TILE, D), f32)],
        compiler_params=pltpu.CompilerParams(dimension_semantics=("arbitrary",), vmem_limit_bytes=VMEM_LIMIT),
        name="outproj_ffn",
    )(x2, a_t, b_t, wo, norm_mix_post, norm_ffn_pre, wg, wu, wd, norm_ffn_post)
    return out.reshape(B, S, D)
```

```python
import math

import numpy as np
import jax
import jax.numpy as jnp
from jax import lax
from jax.experimental import pallas as pl
from jax.experimental.pallas import tpu as pltpu

D_MODEL = 1024
HEAD_DIM = 64
N_HEADS = 8
N_KV = 2
GROUP = N_HEADS // N_KV
WIDTH = N_HEADS * HEAD_DIM
KV_WIDTH = N_KV * HEAD_DIM
WINDOW = 128
GRID_W = 64
ROPE_THETA = 10000.0
EPS = 1e-6
D_FF = 2816
LOG2E = math.log2(math.e)
SCORE_SCALE = HEAD_DIM ** -0.5 * LOG2E

LANES = 128
ONES_ROWS = 16
VT_ROWS = HEAD_DIM + ONES_ROWS
ROW_TILE = 512
Q_TILE = 128
KEY_CHUNK = 512
WINDOW_SLOTS = (0, 2, 5, 9)
FF_CHUNK = 256
VMEM_LIMIT = 56 * 1024 * 1024

_Q_HEAD_ORDER = tuple(h for j in range(GROUP) for h in (j, j + GROUP))


def _rms(x, gain):
    return x * lax.rsqrt(jnp.mean(x * x, axis=-1, keepdims=True) + EPS) * gain


def _rope_block(t, cos, sin_signed):
    lane = lax.broadcasted_iota(jnp.int32, t.shape, 1)
    first_half = (lane % HEAD_DIM) < (HEAD_DIM // 2)
    rot = jnp.where(first_half, pltpu.roll(t, LANES - HEAD_DIM // 2, 1), pltpu.roll(t, HEAD_DIM // 2, 1))
    return t * cos + rot * sin_signed


def _proj_kernel(x_ref, gpre_ref, w_ref, gqk_ref, bd_ref,
                 cos_a_ref, sin_a_ref, cos_bq_ref, sin_bq_ref, cos_bk_ref, sin_bk_ref,
                 qa_ref, ka_ref, vat_ref, qb_ref, kb_ref, vbt_ref):
    x = x_ref[0]
    h = _rms(x, gpre_ref[...]).astype(jnp.bfloat16)
    proj = jnp.dot(h, w_ref[...], preferred_element_type=jnp.float32)
    tm = x.shape[0]
    n_qk = (WIDTH + KV_WIDTH) // LANES

    cos_a = cos_a_ref[...]
    sin_a = sin_a_ref[...]
    for j in range(n_qk):
        t = proj[:, j * LANES:(j + 1) * LANES]
        ssq = jnp.dot((t * t).astype(jnp.bfloat16), bd_ref[...], preferred_element_type=jnp.float32)
        y = t * lax.rsqrt(ssq * (1.0 / HEAD_DIM) + EPS) * gqk_ref[:, j * LANES:(j + 1) * LANES]
        r = _rope_block(y, cos_a, sin_a).astype(jnp.bfloat16)
        if j < n_qk - 1:
            qa_ref[0, :, j * LANES:(j + 1) * LANES] = r
        else:
            ka_ref[0] = r

    off = WIDTH + KV_WIDTH
    for j in range(n_qk):
        t = proj[:, off + j * LANES: off + (j + 1) * LANES]
        if j < n_qk - 1:
            qb_ref[0, :, j * LANES:(j + 1) * LANES] = _rope_block(t, cos_bq_ref[...], sin_bq_ref[...]).astype(jnp.bfloat16)
        else:
            kb_ref[0] = _rope_block(t, cos_bk_ref[...], sin_bk_ref[...]).astype(jnp.bfloat16)

    for idx, (vt_ref, width) in enumerate(((vat_ref, tm), (vbt_ref, Q_TILE))):
        o = 2 * off + idx * KV_WIDTH
        vt = proj[:, o:o + KV_WIDTH].T.astype(jnp.bfloat16)
        ones = jnp.ones((ONES_ROWS, width), jnp.bfloat16)
        for blk in range(tm // width):
            for kvh in range(N_KV):
                base = kvh * VT_ROWS
                vt_ref[0, blk, base:base + HEAD_DIM, :] = vt[kvh * HEAD_DIM:(kvh + 1) * HEAD_DIM,
                                                             blk * width:(blk + 1) * width]
                vt_ref[0, blk, base + HEAD_DIM:base + VT_ROWS, :] = ones


def _masked_queries(q_ref, kvh):
    tq = q_ref.shape[1]
    lane = lax.broadcasted_iota(jnp.int32, (tq, LANES), 1)
    keep = (lane < HEAD_DIM) if kvh == 0 else (lane >= HEAD_DIM)
    blocks = [jnp.where(keep, q_ref[0, :, j * LANES:(j + 1) * LANES], jnp.zeros((), jnp.bfloat16))
              for j in range(GROUP)]
    return jnp.concatenate(blocks, axis=0)


def _group_norm_store(head_outs, gain_ref, o_ref):
    full = jnp.concatenate(head_outs, axis=0)
    ms = jnp.mean(full * full, axis=0, keepdims=True)
    o_ref[0] = (full * lax.rsqrt(ms + EPS) * gain_ref[...]).T.astype(o_ref.dtype)


_NT = (((1,), (1,)), ((), ()))


def _attn_kernel(q_ref, qn_ref, k_ref, kn_ref, vt_ref, gain_a_ref,
                 qb_ref, kb0_ref, kb1_ref, kb2_ref, vb0_ref, vb1_ref, vb2_ref, bias_lo_ref, bias_hi_ref,
                 sink_ref, gain_b_ref, oa_ref, ob_ref, s_ref, m_ref, sw_ref):
    n_chunks = k_ref.shape[1] // KEY_CHUNK
    tq = q_ref.shape[1]
    n_cols = GROUP * tq
    neg_inf = jnp.full((1, n_cols), -jnp.inf, jnp.float32)

    def scores(c, m, qx, keys_ref, buf):
        start = pl.multiple_of(c * KEY_CHUNK, KEY_CHUNK)
        s = lax.dot_general(keys_ref[0, pl.ds(start, KEY_CHUNK), :], qx, _NT,
                            preferred_element_type=jnp.float32)
        s_ref[buf, pl.ds(start, KEY_CHUNK), :] = s
        return jnp.maximum(m, jnp.max(s, axis=0, keepdims=True))

    def weighted(c, acc, m, buf):
        start = pl.multiple_of(c * KEY_CHUNK, KEY_CHUNK)
        p = jnp.exp2(s_ref[buf, pl.ds(start, KEY_CHUNK), :] - m).astype(jnp.bfloat16)
        vt = vt_ref[0, c, buf * VT_ROWS:(buf + 1) * VT_ROWS, :]
        return acc + jnp.dot(vt, p, preferred_element_type=jnp.float32)

    @pl.when((pl.program_id(0) == 0) & (pl.program_id(1) == 0))
    def _():
        qx = _masked_queries(q_ref, 0)
        m_ref[0] = lax.fori_loop(0, n_chunks, lambda c, m: scores(c, m, qx, k_ref, 0), neg_inf)

    window_plan = {}
    for slot, event in zip(WINDOW_SLOTS, (("scores", 0), ("scores", 1), ("weighted", 0), ("weighted", 1))):
        window_plan.setdefault(min(slot, N_KV * n_chunks - 1), []).append(event)
    window_max = [None] * N_KV
    window_outs = [None] * N_HEADS

    head_outs = [None] * N_HEADS
    stages = ((1, q_ref, k_ref), (0, qn_ref, kn_ref))
    for kvh, (nxt, nq_ref, nk_ref) in enumerate(stages):
        qx = _masked_queries(nq_ref, nxt)
        m = m_ref[kvh]
        m_next = neg_inf
        acc = jnp.zeros((VT_ROWS, n_cols), jnp.float32)
        for c in range(n_chunks):
            m_next = scores(c, m_next, qx, nk_ref, nxt)
            acc = weighted(c, acc, m, kvh)
            for kind, wk in window_plan.get(kvh * n_chunks + c, ()):
                if kind == "scores":
                    window_max[wk] = _window_scores(qb_ref, (kb0_ref, kb1_ref, kb2_ref), bias_lo_ref, bias_hi_ref,
                                                    sink_ref, wk, sw_ref)
                else:
                    window_outs[wk * GROUP:(wk + 1) * GROUP] = _window_weighted(
                        (vb0_ref, vb1_ref, vb2_ref), sink_ref, wk, sw_ref, window_max[wk])
        m_ref[nxt] = m_next
        out = acc[:HEAD_DIM] / acc[HEAD_DIM:HEAD_DIM + 1]
        head_outs[kvh * GROUP:(kvh + 1) * GROUP] = [out[:, j * tq:(j + 1) * tq] for j in range(GROUP)]
    _group_norm_store(window_outs, gain_b_ref, ob_ref)
    _group_norm_store(head_outs, gain_a_ref, oa_ref)


def _window_scores(q_ref, k_refs, bias_lo_ref, bias_hi_ref, sink_ref, kvh, sw_ref):
    tq = q_ref.shape[1]
    qx = _masked_queries(q_ref, kvh)
    biases = (bias_lo_ref, None, bias_hi_ref)
    m = sink_ref[kvh:kvh + 1, :]
    for i, (k_ref, bias_ref) in enumerate(zip(k_refs, biases)):
        s = lax.dot_general(k_ref[0], qx, _NT, preferred_element_type=jnp.float32)
        if bias_ref is not None:
            s = s + bias_ref[0]
        sw_ref[kvh, i * tq:(i + 1) * tq, :] = s
        m = jnp.maximum(m, jnp.max(s, axis=0, keepdims=True))
    return m


def _window_weighted(v_refs, sink_ref, kvh, sw_ref, m):
    tq = v_refs[0].shape[3]
    p = jnp.exp2(sw_ref[kvh] - m).astype(jnp.bfloat16)
    rs = slice(kvh * VT_ROWS, (kvh + 1) * VT_ROWS)
    vt = jnp.concatenate([v_ref[0, 0, rs, :] for v_ref in v_refs], axis=1)
    acc = jnp.dot(vt, p, preferred_element_type=jnp.float32)
    den = acc[HEAD_DIM:HEAD_DIM + 1] + jnp.exp2(sink_ref[kvh:kvh + 1, :] - m)
    out = acc[:HEAD_DIM] / den
    return [out[:, j * tq:(j + 1) * tq] for j in range(GROUP)]


def _post_kernel(x_ref, a_ref, b_ref, wo_ref, gpost_ref, gffn_ref, wg_ref, wu_ref, wd_ref, gfpost_ref,
                 o_ref, acc_ref):
    mixed = jnp.dot(a_ref[...], wo_ref[:WIDTH, :], preferred_element_type=jnp.float32)
    mixed = mixed + jnp.dot(b_ref[...], wo_ref[WIDTH:, :], preferred_element_type=jnp.float32)
    x1 = x_ref[...] + _rms(mixed, gpost_ref[...])
    h = _rms(x1, gffn_ref[...]).astype(jnp.bfloat16)
    acc_ref[...] = jnp.zeros_like(acc_ref)
    for c in range(D_FF // FF_CHUNK):
        cols = slice(c * FF_CHUNK, (c + 1) * FF_CHUNK)
        g = jnp.dot(h, wg_ref[:, cols], preferred_element_type=jnp.float32)
        u = jnp.dot(h, wu_ref[:, cols], preferred_element_type=jnp.float32)
        a = (g / (1.0 + jnp.exp(-g)) * u).astype(jnp.bfloat16)
        acc_ref[...] += jnp.dot(a, wd_ref[cols, :], preferred_element_type=jnp.float32)
    o_ref[...] = x1 + _rms(acc_ref[...], gfpost_ref[...])


def _rope_tables(seq_len):
    rows = seq_len // GRID_W
    t = np.arange(seq_len, dtype=np.float32)
    row = np.repeat(np.arange(rows, dtype=np.float32), GRID_W)
    col = np.tile(np.arange(GRID_W, dtype=np.float32), rows)
    ax_pairs = HEAD_DIM // 4
    freq_ax = jnp.asarray(ROPE_THETA, jnp.float32) ** (-jnp.arange(ax_pairs, dtype=jnp.float32) / ax_pairs)
    ang_axial = jnp.concatenate([row[:, None] * freq_ax[None, :], col[:, None] * freq_ax[None, :]], axis=-1)
    n_pairs = HEAD_DIM // 2
    freq_1d = jnp.asarray(ROPE_THETA, jnp.float32) ** (-jnp.arange(n_pairs, dtype=jnp.float32) / n_pairs)
    ang_1d = t[:, None] * freq_1d[None, :]

    def tables(ang, scale):
        cos = jnp.cos(ang) * scale
        sin = jnp.sin(ang) * scale
        cos_head = jnp.concatenate([cos, cos], axis=-1)
        sin_head = jnp.concatenate([-sin, sin], axis=-1)
        return jnp.tile(cos_head, (1, LANES // HEAD_DIM)), jnp.tile(sin_head, (1, LANES // HEAD_DIM))

    return tables(ang_axial, 1.0), tables(ang_1d, SCORE_SCALE), tables(ang_1d, 1.0)


def _const_spec(shape):
    return pl.BlockSpec(shape, lambda *_: (0,) * len(shape), pipeline_mode=pl.Buffered(1))


def kernel(x, norm_mix_pre, w_in, q_norm_a, k_norm_a, sink_b, group_norm_a, group_norm_b, w_out,
           norm_mix_post, norm_ffn_pre, w_gate, w_up, w_down, norm_ffn_post):
    B, S, D = x.shape
    assert D == D_MODEL and S % ROW_TILE == 0 and S % GRID_W == 0
    f32, bf16 = jnp.float32, jnp.bfloat16
    n_row_tiles = S // ROW_TILE
    n_q_tiles = S // Q_TILE
    sub = ROW_TILE // Q_TILE

    w = w_in[0]
    q_cols = np.concatenate([np.arange(h * HEAD_DIM, (h + 1) * HEAD_DIM) for h in _Q_HEAD_ORDER])
    o_ka, o_va, o_qb, o_kb, o_vb = WIDTH, WIDTH + KV_WIDTH, WIDTH + 2 * KV_WIDTH, 2 * WIDTH + 2 * KV_WIDTH, 2 * WIDTH + 3 * KV_WIDTH
    col_idx = np.concatenate([q_cols, o_ka + np.arange(KV_WIDTH), o_qb + q_cols, o_kb + np.arange(KV_WIDTH),
                              o_va + np.arange(KV_WIDTH), o_vb + np.arange(KV_WIDTH)])
    w_p = w[:, col_idx].astype(bf16)
    gqk = jnp.concatenate([jnp.tile(q_norm_a[0] * SCORE_SCALE, N_HEADS), jnp.tile(k_norm_a[0], N_KV)])[None, :]
    head_of_lane = np.arange(LANES) // HEAD_DIM
    block_diag = jnp.asarray(head_of_lane[:, None] == head_of_lane[None, :], bf16)
    (cos_a, sin_a), (cos_bq, sin_bq), (cos_bk, sin_bk) = _rope_tables(S)

    table_spec = pl.BlockSpec((ROW_TILE, LANES), lambda i, b: (i, 0))
    q_out = pl.BlockSpec((1, ROW_TILE, WIDTH), lambda i, b: (b, i, 0))
    k_out = pl.BlockSpec((1, ROW_TILE, KV_WIDTH), lambda i, b: (b, i, 0))
    vat_out = pl.BlockSpec((1, 1, N_KV * VT_ROWS, ROW_TILE), lambda i, b: (b, i, 0, 0))
    vbt_out = pl.BlockSpec((1, sub, N_KV * VT_ROWS, Q_TILE), lambda i, b: (b, i, 0, 0))
    qa, ka, vat, qb, kb, vbt = pl.pallas_call(
        _proj_kernel,
        grid=(n_row_tiles, B),
        in_specs=[pl.BlockSpec((1, ROW_TILE, D), lambda i, b: (b, i, 0)),
                  _const_spec((1, D)), _const_spec(w_p.shape), _const_spec(gqk.shape), _const_spec((LANES, LANES)),
                  table_spec, table_spec, table_spec, table_spec, table_spec, table_spec],
        out_specs=[q_out, k_out, vat_out, q_out, k_out, vbt_out],
        out_shape=[jax.ShapeDtypeStruct((B, S, WIDTH), bf16), jax.ShapeDtypeStruct((B, S, KV_WIDTH), bf16),
                   jax.ShapeDtypeStruct((B, n_row_tiles, N_KV * VT_ROWS, ROW_TILE), bf16),
                   jax.ShapeDtypeStruct((B, S, WIDTH), bf16), jax.ShapeDtypeStruct((B, S, KV_WIDTH), bf16),
                   jax.ShapeDtypeStruct((B, n_q_tiles, N_KV * VT_ROWS, Q_TILE), bf16)],
        compiler_params=pltpu.CompilerParams(dimension_semantics=("arbitrary", "arbitrary"),
                                             vmem_limit_bytes=VMEM_LIMIT),
        name="proj_rope",
    )(x, norm_mix_pre, w_p, gqk, block_diag, cos_a, sin_a, cos_bq, sin_bq, cos_bk, sin_bk)

    gain_a = jnp.broadcast_to(group_norm_a[0][:, None], (WIDTH, Q_TILE))
    gain_b = jnp.broadcast_to(group_norm_b[0][:, None], (WIDTH, Q_TILE))
    sink_cols = jnp.repeat(sink_b[0] * LOG2E, Q_TILE).reshape(N_KV, GROUP * Q_TILE)
    key_row = np.arange(Q_TILE)[:, None]
    q_col = np.arange(GROUP * Q_TILE)[None, :] % Q_TILE
    neg = np.float32(-np.inf)
    band_bias = jnp.asarray(np.stack([np.where(q_col <= key_row, np.float32(0), neg),
                                      np.where(key_row <= q_col, np.float32(0), neg),
                                      np.full((Q_TILE, GROUP * Q_TILE), neg)]).astype(np.float32))

    out_spec = pl.BlockSpec((1, Q_TILE, WIDTH), lambda b, t: (b, t, 0))
    q_spec = pl.BlockSpec((1, Q_TILE, WIDTH), lambda b, t: (b, t, 0))
    last_tile = B * n_q_tiles - 1

    def next_tile(b, t):
        nxt = jnp.minimum(b * n_q_tiles + t + 1, last_tile)
        return nxt // n_q_tiles, nxt % n_q_tiles

    def k_band_spec(d):
        return pl.BlockSpec((1, Q_TILE, KV_WIDTH), lambda b, t: (b, jnp.clip(t + d, 0, n_q_tiles - 1), 0))

    def v_band_spec(d):
        return pl.BlockSpec((1, 1, N_KV * VT_ROWS, Q_TILE),
                            lambda b, t: (b, jnp.clip(t + d, 0, n_q_tiles - 1), 0, 0))

    bias_block = (1, Q_TILE, GROUP * Q_TILE)
    attn_a, attn_b = pl.pallas_call(
        _attn_kernel,
        grid=(B, n_q_tiles),
        in_specs=[q_spec,
                  pl.BlockSpec((1, Q_TILE, WIDTH), lambda b, t: (*next_tile(b, t), 0)),
                  pl.BlockSpec((1, S, KV_WIDTH), lambda b, t: (b, 0, 0)),
                  pl.BlockSpec((1, S, KV_WIDTH), lambda b, t: (next_tile(b, t)[0], 0, 0)),
                  pl.BlockSpec((1, n_row_tiles, N_KV * VT_ROWS, ROW_TILE), lambda b, t: (b, 0, 0, 0)),
                  _const_spec((WIDTH, Q_TILE)),
                  q_spec, k_band_spec(-1), k_band_spec(0), k_band_spec(1),
                  v_band_spec(-1), v_band_spec(0), v_band_spec(1),
                  pl.BlockSpec(bias_block, lambda b, t: (jnp.where(t == 0, 2, 0), 0, 0)),
                  pl.BlockSpec(bias_block, lambda b, t: (jnp.where(t == n_q_tiles - 1, 2, 1), 0, 0)),
                  _const_spec(sink_cols.shape), _const_spec((WIDTH, Q_TILE))],
        out_specs=[out_spec, out_spec],
        out_shape=[jax.ShapeDtypeStruct((B, S, WIDTH), bf16), jax.ShapeDtypeStruct((B, S, WIDTH), bf16)],
        scratch_shapes=[pltpu.VMEM((N_KV, S, GROUP * Q_TILE), f32),
                        pltpu.VMEM((N_KV, 1, GROUP * Q_TILE), f32),
                        pltpu.VMEM((N_KV, 3 * Q_TILE, GROUP * Q_TILE), f32)],
        compiler_params=pltpu.CompilerParams(dimension_semantics=("arbitrary", "arbitrary"),
                                             vmem_limit_bytes=VMEM_LIMIT),
        name="attention",
    )(qa, qa, ka, ka, vat, gain_a, qb, kb, kb, kb, vbt, vbt, vbt, band_bias, band_bias, sink_cols, gain_b)

    wg = w_gate[0].astype(bf16)
    wu = w_up[0].astype(bf16)
    wd = w_down[0].astype(bf16)
    wo = w_out[0].astype(bf16)
    row_spec = pl.BlockSpec((ROW_TILE, D), lambda i: (i, 0))
    mix_spec = pl.BlockSpec((ROW_TILE, WIDTH), lambda i: (i, 0))
    out = pl.pallas_call(
        _post_kernel,
        grid=(B * n_row_tiles,),
        in_specs=[row_spec, mix_spec, mix_spec, _const_spec(wo.shape), _const_spec((1, D)), _const_spec((1, D)),
                  _const_spec(wg.shape), _const_spec(wu.shape), _const_spec(wd.shape), _const_spec((1, D))],
        out_specs=row_spec,
        out_shape=jax.ShapeDtypeStruct((B * S, D), f32),
        scratch_shapes=[pltpu.VMEM((ROW_TILE, D), f32)],
        compiler_params=pltpu.CompilerParams(dimension_semantics=("arbitrary",), vmem_limit_bytes=VMEM_LIMIT),
        name="outproj_ffn",
    )(x.reshape(B * S, D), attn_a.reshape(B * S, WIDTH), attn_b.reshape(B * S, WIDTH), wo,
      norm_mix_post, norm_ffn_pre, wg, wu, wd, norm_ffn_post)
    return out.reshape(B, S, D)
```

```python
import math

import numpy as np
import jax
import jax.numpy as jnp
from jax import lax
from jax.experimental import pallas as pl
from jax.experimental.pallas import tpu as pltpu

D_MODEL = 1024
HEAD_DIM = 64
N_HEADS = 8
N_KV = 2
GROUP = N_HEADS // N_KV
WIDTH = N_HEADS * HEAD_DIM
KV_WIDTH = N_KV * HEAD_DIM
WINDOW = 128
GRID_W = 64
ROPE_THETA = 10000.0
EPS = 1e-6
D_FF = 2816
LOG2E = math.log2(math.e)
SCORE_SCALE = HEAD_DIM ** -0.5 * LOG2E

LANES = 128
MXU_COLS = 256
ONES_ROWS = 16
VT_ROWS = HEAD_DIM + ONES_ROWS
ROW_TILE = 512
Q_TILE = 128
KEY_CHUNK = 256
WINDOW_SLOTS = (0, 2, 5, 9)
FF_CHUNK = 256
VMEM_LIMIT = 56 * 1024 * 1024

_Q_HEAD_ORDER = tuple(h for j in range(GROUP) for h in (j, j + GROUP))


def _rms(x, gain):
    return x * lax.rsqrt(jnp.mean(x * x, axis=-1, keepdims=True) + EPS) * gain


def _rope_block(t, cos, sin_signed):
    lane = lax.broadcasted_iota(jnp.int32, t.shape, 1)
    first_half = (lane % HEAD_DIM) < (HEAD_DIM // 2)
    rot = jnp.where(first_half, pltpu.roll(t, LANES - HEAD_DIM // 2, 1), pltpu.roll(t, HEAD_DIM // 2, 1))
    return t * cos + rot * sin_signed


def _proj_kernel(x_ref, gpre_ref, w_ref, gqk_ref, bd_ref,
                 cos_a_ref, sin_a_ref, cos_bq_ref, sin_bq_ref, cos_bk_ref, sin_bk_ref,
                 qa_ref, ka_ref, vat_ref, qb_ref, kb_ref, vbt_ref):
    x = x_ref[0]
    h = _rms(x, gpre_ref[...]).astype(jnp.bfloat16)
    tm = x.shape[0]
    half = w_ref.shape[1] // 2
    n_qk = (WIDTH + KV_WIDTH) // LANES
    proj_1 = jnp.dot(h, w_ref[:, :half], preferred_element_type=jnp.float32)
    proj_2 = jnp.dot(h, w_ref[:, half:], preferred_element_type=jnp.float32)

    cos_a = cos_a_ref[...]
    sin_a = sin_a_ref[...]
    for j0 in range(0, n_qk, MXU_COLS // LANES):
        nb = min(MXU_COLS // LANES, n_qk - j0)
        t = proj_1[:, j0 * LANES:(j0 + nb) * LANES]
        ssq = jnp.dot((t * t).astype(jnp.bfloat16), bd_ref[:nb * LANES, :nb * LANES],
                      preferred_element_type=jnp.float32)
        y = t * lax.rsqrt(ssq * (1.0 / HEAD_DIM) + EPS) * gqk_ref[:, j0 * LANES:(j0 + nb) * LANES]
        for j in range(j0, j0 + nb):
            r = _rope_block(y[:, (j - j0) * LANES:(j - j0 + 1) * LANES], cos_a, sin_a).astype(jnp.bfloat16)
            if j < n_qk - 1:
                qa_ref[0, :, j * LANES:(j + 1) * LANES] = r
            else:
                ka_ref[0] = r

    kb_ref[0] = _rope_block(proj_1[:, n_qk * LANES:], cos_bk_ref[...], sin_bk_ref[...]).astype(jnp.bfloat16)
    for j in range(WIDTH // LANES):
        t = proj_2[:, j * LANES:(j + 1) * LANES]
        qb_ref[0, :, j * LANES:(j + 1) * LANES] = _rope_block(t, cos_bq_ref[...], sin_bq_ref[...]).astype(jnp.bfloat16)

    for idx, (vt_ref, width) in enumerate(((vat_ref, KEY_CHUNK), (vbt_ref, Q_TILE))):
        o = WIDTH + idx * KV_WIDTH
        vt = proj_2[:, o:o + KV_WIDTH].T.astype(jnp.bfloat16)
        ones = jnp.ones((ONES_ROWS, width), jnp.bfloat16)
        for blk in range(tm // width):
            for kvh in range(N_KV):
                base = kvh * VT_ROWS
                vt_ref[0, blk, base:base + HEAD_DIM, :] = vt[kvh * HEAD_DIM:(kvh + 1) * HEAD_DIM,
                                                             blk * width:(blk + 1) * width]
                vt_ref[0, blk, base + HEAD_DIM:base + VT_ROWS, :] = ones


def _masked_queries(q_ref, kvh):
    tq = q_ref.shape[1]
    lane = lax.broadcasted_iota(jnp.int32, (tq, LANES), 1)
    keep = (lane < HEAD_DIM) if kvh == 0 else (lane >= HEAD_DIM)
    blocks = [jnp.where(keep, q_ref[0, :, j * LANES:(j + 1) * LANES], jnp.zeros((), jnp.bfloat16))
              for j in range(GROUP)]
    return jnp.concatenate(blocks, axis=0)


def _group_norm_store(head_outs, gain_ref, o_ref):
    full = jnp.concatenate(head_outs, axis=0)
    ms = jnp.mean(full * full, axis=0, keepdims=True)
    o_ref[0] = (full * lax.rsqrt(ms + EPS) * gain_ref[...]).T.astype(o_ref.dtype)


_NT = (((1,), (1,)), ((), ()))


def _attn_kernel(q_ref, qn_ref, k_ref, kn_ref, vt_ref, gain_a_ref,
                 qb_ref, kb0_ref, kb1_ref, kb2_ref, vb0_ref, vb1_ref, vb2_ref, bias_lo_ref, bias_hi_ref,
                 sink_ref, gain_b_ref, oa_ref, ob_ref, s_ref, m_ref, sw_ref):
    n_chunks = k_ref.shape[1] // KEY_CHUNK
    tq = q_ref.shape[1]
    n_cols = GROUP * tq
    neg_inf = jnp.full((1, n_cols), -jnp.inf, jnp.float32)

    def scores(c, m, qx, keys_ref, buf):
        start = pl.multiple_of(c * KEY_CHUNK, KEY_CHUNK)
        s = lax.dot_general(keys_ref[0, pl.ds(start, KEY_CHUNK), :], qx, _NT,
                            preferred_element_type=jnp.float32)
        s_ref[buf, pl.ds(start, KEY_CHUNK), :] = s
        return jnp.maximum(m, jnp.max(s, axis=0, keepdims=True))

    def weighted(c, acc, m, buf):
        start = pl.multiple_of(c * KEY_CHUNK, KEY_CHUNK)
        p = jnp.exp2(s_ref[buf, pl.ds(start, KEY_CHUNK), :] - m).astype(jnp.bfloat16)
        vt = vt_ref[0, c, buf * VT_ROWS:(buf + 1) * VT_ROWS, :]
        return acc + jnp.dot(vt, p, preferred_element_type=jnp.float32)

    @pl.when((pl.program_id(0) == 0) & (pl.program_id(1) == 0))
    def _():
        qx = _masked_queries(q_ref, 0)
        m_ref[0] = lax.fori_loop(0, n_chunks, lambda c, m: scores(c, m, qx, k_ref, 0), neg_inf)

    window_plan = {}
    for slot, event in zip(WINDOW_SLOTS, (("scores", 0), ("scores", 1), ("weighted", 0), ("weighted", 1))):
        window_plan.setdefault(min(slot * n_chunks // 8, N_KV * n_chunks - 1), []).append(event)
    window_max = [None] * N_KV
    window_outs = [None] * N_HEADS

    head_outs = [None] * N_HEADS
    stages = ((1, q_ref, k_ref), (0, qn_ref, kn_ref))
    for kvh, (nxt, nq_ref, nk_ref) in enumerate(stages):
        qx = _masked_queries(nq_ref, nxt)
        m = m_ref[kvh]
        m_next = neg_inf
        acc = jnp.zeros((VT_ROWS, n_cols), jnp.float32)
        for c in range(n_chunks):
            m_next = scores(c, m_next, qx, nk_ref, nxt)
            acc = weighted(c, acc, m, kvh)
            for kind, wk in window_plan.get(kvh * n_chunks + c, ()):
                if kind == "scores":
                    window_max[wk] = _window_scores(qb_ref, (kb0_ref, kb1_ref, kb2_ref), bias_lo_ref, bias_hi_ref,
                                                    sink_ref, wk, sw_ref)
                else:
                    window_outs[wk * GROUP:(wk + 1) * GROUP] = _window_weighted(
                        (vb0_ref, vb1_ref, vb2_ref), sink_ref, wk, sw_ref, window_max[wk])
        m_ref[nxt] = m_next
        out = acc[:HEAD_DIM] / acc[HEAD_DIM:HEAD_DIM + 1]
        head_outs[kvh * GROUP:(kvh + 1) * GROUP] = [out[:, j * tq:(j + 1) * tq] for j in range(GROUP)]
    _group_norm_store(window_outs, gain_b_ref, ob_ref)
    _group_norm_store(head_outs, gain_a_ref, oa_ref)


def _window_scores(q_ref, k_refs, bias_lo_ref, bias_hi_ref, sink_ref, kvh, sw_ref):
    tq = q_ref.shape[1]
    qx = _masked_queries(q_ref, kvh)
    biases = (bias_lo_ref, None, bias_hi_ref)
    m = sink_ref[kvh:kvh + 1, :]
    for i, (k_ref, bias_ref) in enumerate(zip(k_refs, biases)):
        s = lax.dot_general(k_ref[0], qx, _NT, preferred_element_type=jnp.float32)
        if bias_ref is not None:
            s = s + bias_ref[0]
        sw_ref[kvh, i * tq:(i + 1) * tq, :] = s
        m = jnp.maximum(m, jnp.max(s, axis=0, keepdims=True))
    return m


def _window_weighted(v_refs, sink_ref, kvh, sw_ref, m):
    tq = v_refs[0].shape[3]
    p = jnp.exp2(sw_ref[kvh] - m).astype(jnp.bfloat16)
    rs = slice(kvh * VT_ROWS, (kvh + 1) * VT_ROWS)
    vt = jnp.concatenate([v_ref[0, 0, rs, :] for v_ref in v_refs], axis=1)
    acc = jnp.dot(vt, p, preferred_element_type=jnp.float32)
    den = acc[HEAD_DIM:HEAD_DIM + 1] + jnp.exp2(sink_ref[kvh:kvh + 1, :] - m)
    out = acc[:HEAD_DIM] / den
    return [out[:, j * tq:(j + 1) * tq] for j in range(GROUP)]


def _post_kernel(x_ref, a_ref, b_ref, wo_ref, gpost_ref, gffn_ref, wg_ref, wu_ref, wd_ref, gfpost_ref,
                 o_ref, acc_ref):
    mixed = jnp.dot(a_ref[...], wo_ref[:WIDTH, :], preferred_element_type=jnp.float32)
    mixed = mixed + jnp.dot(b_ref[...], wo_ref[WIDTH:, :], preferred_element_type=jnp.float32)
    x1 = x_ref[...] + _rms(mixed, gpost_ref[...])
    h = _rms(x1, gffn_ref[...]).astype(jnp.bfloat16)
    acc_ref[...] = jnp.zeros_like(acc_ref)
    for c in range(D_FF // FF_CHUNK):
        cols = slice(c * FF_CHUNK, (c + 1) * FF_CHUNK)
        g = jnp.dot(h, wg_ref[:, cols], preferred_element_type=jnp.float32)
        u = jnp.dot(h, wu_ref[:, cols], preferred_element_type=jnp.float32)
        a = (g / (1.0 + jnp.exp(-g)) * u).astype(jnp.bfloat16)
        acc_ref[...] += jnp.dot(a, wd_ref[cols, :], preferred_element_type=jnp.float32)
    o_ref[...] = x1 + _rms(acc_ref[...], gfpost_ref[...])


def _rope_tables(seq_len):
    rows = seq_len // GRID_W
    t = np.arange(seq_len, dtype=np.float32)
    row = np.repeat(np.arange(rows, dtype=np.float32), GRID_W)
    col = np.tile(np.arange(GRID_W, dtype=np.float32), rows)
    ax_pairs = HEAD_DIM // 4
    freq_ax = jnp.asarray(ROPE_THETA, jnp.float32) ** (-jnp.arange(ax_pairs, dtype=jnp.float32) / ax_pairs)
    ang_axial = jnp.concatenate([row[:, None] * freq_ax[None, :], col[:, None] * freq_ax[None, :]], axis=-1)
    n_pairs = HEAD_DIM // 2
    freq_1d = jnp.asarray(ROPE_THETA, jnp.float32) ** (-jnp.arange(n_pairs, dtype=jnp.float32) / n_pairs)
    ang_1d = t[:, None] * freq_1d[None, :]

    def tables(ang, scale):
        cos = jnp.cos(ang) * scale
        sin = jnp.sin(ang) * scale
        cos_head = jnp.concatenate([cos, cos], axis=-1)
        sin_head = jnp.concatenate([-sin, sin], axis=-1)
        return jnp.tile(cos_head, (1, LANES // HEAD_DIM)), jnp.tile(sin_head, (1, LANES // HEAD_DIM))

    return tables(ang_axial, 1.0), tables(ang_1d, SCORE_SCALE), tables(ang_1d, 1.0)


def _const_spec(shape):
    return pl.BlockSpec(shape, lambda *_: (0,) * len(shape), pipeline_mode=pl.Buffered(1))


def kernel(x, norm_mix_pre, w_in, q_norm_a, k_norm_a, sink_b, group_norm_a, group_norm_b, w_out,
           norm_mix_post, norm_ffn_pre, w_gate, w_up, w_down, norm_ffn_post):
    B, S, D = x.shape
    assert D == D_MODEL and S % ROW_TILE == 0 and S % GRID_W == 0
    f32, bf16 = jnp.float32, jnp.bfloat16
    n_row_tiles = S // ROW_TILE
    n_q_tiles = S // Q_TILE
    sub = ROW_TILE // Q_TILE

    w = w_in[0]
    q_cols = np.concatenate([np.arange(h * HEAD_DIM, (h + 1) * HEAD_DIM) for h in _Q_HEAD_ORDER])
    o_ka, o_va, o_qb, o_kb, o_vb = WIDTH, WIDTH + KV_WIDTH, WIDTH + 2 * KV_WIDTH, 2 * WIDTH + 2 * KV_WIDTH, 2 * WIDTH + 3 * KV_WIDTH
    col_idx = np.concatenate([q_cols, o_ka + np.arange(KV_WIDTH), o_kb + np.arange(KV_WIDTH), o_qb + q_cols,
                              o_va + np.arange(KV_WIDTH), o_vb + np.arange(KV_WIDTH)])
    w_p = w[:, col_idx].astype(bf16)
    gqk = jnp.concatenate([jnp.tile(q_norm_a[0] * SCORE_SCALE, N_HEADS), jnp.tile(k_norm_a[0], N_KV)])[None, :]
    head_of_lane = np.arange(MXU_COLS) // HEAD_DIM
    block_diag = jnp.asarray(head_of_lane[:, None] == head_of_lane[None, :], bf16)
    (cos_a, sin_a), (cos_bq, sin_bq), (cos_bk, sin_bk) = _rope_tables(S)

    table_spec = pl.BlockSpec((ROW_TILE, LANES), lambda i, b: (i, 0))
    q_out = pl.BlockSpec((1, ROW_TILE, WIDTH), lambda i, b: (b, i, 0))
    k_out = pl.BlockSpec((1, ROW_TILE, KV_WIDTH), lambda i, b: (b, i, 0))
    vat_out = pl.BlockSpec((1, ROW_TILE // KEY_CHUNK, N_KV * VT_ROWS, KEY_CHUNK), lambda i, b: (b, i, 0, 0))
    vbt_out = pl.BlockSpec((1, sub, N_KV * VT_ROWS, Q_TILE), lambda i, b: (b, i, 0, 0))
    qa, ka, vat, qb, kb, vbt = pl.pallas_call(
        _proj_kernel,
        grid=(n_row_tiles, B),
        in_specs=[pl.BlockSpec((1, ROW_TILE, D), lambda i, b: (b, i, 0)),
                  _const_spec((1, D)), _const_spec(w_p.shape), _const_spec(gqk.shape), _const_spec((MXU_COLS, MXU_COLS)),
                  table_spec, table_spec, table_spec, table_spec, table_spec, table_spec],
        out_specs=[q_out, k_out, vat_out, q_out, k_out, vbt_out],
        out_shape=[jax.ShapeDtypeStruct((B, S, WIDTH), bf16), jax.ShapeDtypeStruct((B, S, KV_WIDTH), bf16),
                   jax.ShapeDtypeStruct((B, S // KEY_CHUNK, N_KV * VT_ROWS, KEY_CHUNK), bf16),
                   jax.ShapeDtypeStruct((B, S, WIDTH), bf16), jax.ShapeDtypeStruct((B, S, KV_WIDTH), bf16),
                   jax.ShapeDtypeStruct((B, n_q_tiles, N_KV * VT_ROWS, Q_TILE), bf16)],
        compiler_params=pltpu.CompilerParams(dimension_semantics=("arbitrary", "arbitrary"),
                                             vmem_limit_bytes=VMEM_LIMIT),
        name="proj_rope",
    )(x, norm_mix_pre, w_p, gqk, block_diag, cos_a, sin_a, cos_bq, sin_bq, cos_bk, sin_bk)

    gain_a = jnp.broadcast_to(group_norm_a[0][:, None], (WIDTH, Q_TILE))
    gain_b = jnp.broadcast_to(group_norm_b[0][:, None], (WIDTH, Q_TILE))
    sink_cols = jnp.repeat(sink_b[0] * LOG2E, Q_TILE).reshape(N_KV, GROUP * Q_TILE)
    key_row = np.arange(Q_TILE)[:, None]
    q_col = np.arange(GROUP * Q_TILE)[None, :] % Q_TILE
    neg = np.float32(-np.inf)
    band_bias = jnp.asarray(np.stack([np.where(q_col <= key_row, np.float32(0), neg),
                                      np.where(key_row <= q_col, np.float32(0), neg),
                                      np.full((Q_TILE, GROUP * Q_TILE), neg)]).astype(np.float32))

    out_spec = pl.BlockSpec((1, Q_TILE, WIDTH), lambda b, t: (b, t, 0))
    q_spec = pl.BlockSpec((1, Q_TILE, WIDTH), lambda b, t: (b, t, 0))
    last_tile = B * n_q_tiles - 1

    def next_tile(b, t):
        nxt = jnp.minimum(b * n_q_tiles + t + 1, last_tile)
        return nxt // n_q_tiles, nxt % n_q_tiles

    def k_band_spec(d):
        return pl.BlockSpec((1, Q_TILE, KV_WIDTH), lambda b, t: (b, jnp.clip(t + d, 0, n_q_tiles - 1), 0))

    def v_band_spec(d):
        return pl.BlockSpec((1, 1, N_KV * VT_ROWS, Q_TILE),
                            lambda b, t: (b, jnp.clip(t + d, 0, n_q_tiles - 1), 0, 0))

    bias_block = (1, Q_TILE, GROUP * Q_TILE)
    attn_a, attn_b = pl.pallas_call(
        _attn_kernel,
        grid=(B, n_q_tiles),
        in_specs=[q_spec,
                  pl.BlockSpec((1, Q_TILE, WIDTH), lambda b, t: (*next_tile(b, t), 0)),
                  pl.BlockSpec((1, S, KV_WIDTH), lambda b, t: (b, 0, 0)),
                  pl.BlockSpec((1, S, KV_WIDTH), lambda b, t: (next_tile(b, t)[0], 0, 0)),
                  pl.BlockSpec((1, S // KEY_CHUNK, N_KV * VT_ROWS, KEY_CHUNK), lambda b, t: (b, 0, 0, 0)),
                  _const_spec((WIDTH, Q_TILE)),
                  q_spec, k_band_spec(-1), k_band_spec(0), k_band_spec(1),
                  v_band_spec(-1), v_band_spec(0), v_band_spec(1),
                  pl.BlockSpec(bias_block, lambda b, t: (jnp.where(t == 0, 2, 0), 0, 0)),
                  pl.BlockSpec(bias_block, lambda b, t: (jnp.where(t == n_q_tiles - 1, 2, 1), 0, 0)),
                  _const_spec(sink_cols.shape), _const_spec((WIDTH, Q_TILE))],
        out_specs=[out_spec, out_spec],
        out_shape=[jax.ShapeDtypeStruct((B, S, WIDTH), bf16), jax.ShapeDtypeStruct((B, S, WIDTH), bf16)],
        scratch_shapes=[pltpu.VMEM((N_KV, S, GROUP * Q_TILE), f32),
                        pltpu.VMEM((N_KV, 1, GROUP * Q_TILE), f32),
                        pltpu.VMEM((N_KV, 3 * Q_TILE, GROUP * Q_TILE), f32)],
        compiler_params=pltpu.CompilerParams(dimension_semantics=("arbitrary", "arbitrary"),
                                             vmem_limit_bytes=VMEM_LIMIT),
        name="attention",
    )(qa, qa, ka, ka, vat, gain_a, qb, kb, kb, kb, vbt, vbt, vbt, band_bias, band_bias, sink_cols, gain_b)

    wg = w_gate[0].astype(bf16)
    wu = w_up[0].astype(bf16)
    wd = w_down[0].astype(bf16)
    wo = w_out[0].astype(bf16)
    row_spec = pl.BlockSpec((ROW_TILE, D), lambda i: (i, 0))
    mix_spec = pl.BlockSpec((ROW_TILE, WIDTH), lambda i: (i, 0))
    out = pl.pallas_call(
        _post_kernel,
        grid=(B * n_row_tiles,),
        in_specs=[row_spec, mix_spec, mix_spec, _const_spec(wo.shape), _const_spec((1, D)), _const_spec((1, D)),
                  _const_spec(wg.shape), _const_spec(wu.shape), _const_spec(wd.shape), _const_spec((1, D))],
        out_specs=row_spec,
        out_shape=jax.ShapeDtypeStruct((B * S, D), f32),
        scratch_shapes=[pltpu.VMEM((ROW_TILE, D), f32)],
        compiler_params=pltpu.CompilerParams(dimension_semantics=("arbitrary",), vmem_limit_bytes=VMEM_LIMIT),
        name="outproj_ffn",
    )(x.reshape(B * S, D), attn_a.reshape(B * S, WIDTH), attn_b.reshape(B * S, WIDTH), wo,
      norm_mix_post, norm_ffn_pre, wg, wu, wd, norm_ffn_post)
    return out.reshape(B, S, D)
```

```python
import math

import numpy as np
import jax
import jax.numpy as jnp
from jax import lax
from jax.experimental import pallas as pl
from jax.experimental.pallas import tpu as pltpu

D_MODEL = 1024
HEAD_DIM = 64
N_HEADS = 8
N_KV = 2
GROUP = N_HEADS // N_KV
WIDTH = N_HEADS * HEAD_DIM
KV_WIDTH = N_KV * HEAD_DIM
WINDOW = 128
GRID_W = 64
ROPE_THETA = 10000.0
EPS = 1e-6
D_FF = 2816
LOG2E = math.log2(math.e)
SCORE_SCALE = HEAD_DIM ** -0.5 * LOG2E

LANES = 128
MXU_COLS = 256
ONES_ROWS = 64
VT_ROWS = HEAD_DIM + ONES_ROWS
ROW_TILE = 512
Q_TILE = 128
KEY_CHUNK = 512
WINDOW_SLOTS = (0, 2, 5, 9)
FF_CHUNK = 256
VMEM_LIMIT = 56 * 1024 * 1024

_Q_HEAD_ORDER = tuple(h for j in range(GROUP) for h in (j, j + GROUP))


def _rms(x, gain):
    return x * lax.rsqrt(jnp.mean(x * x, axis=-1, keepdims=True) + EPS) * gain


def _rope_block(t, cos, sin_signed):
    lane = lax.broadcasted_iota(jnp.int32, t.shape, 1)
    first_half = (lane % HEAD_DIM) < (HEAD_DIM // 2)
    rot = jnp.where(first_half, pltpu.roll(t, LANES - HEAD_DIM // 2, 1), pltpu.roll(t, HEAD_DIM // 2, 1))
    return t * cos + rot * sin_signed


def _proj_kernel(x_ref, gpre_ref, w_ref, gqk_ref, bd_ref,
                 cos_a_ref, sin_a_ref, cos_bq_ref, sin_bq_ref, cos_bk_ref, sin_bk_ref,
                 qa_ref, ka_ref, vat_ref, qb_ref, kb_ref, vbt_ref):
    x = x_ref[0]
    h = _rms(x, gpre_ref[...]).astype(jnp.bfloat16)
    tm = x.shape[0]
    half = w_ref.shape[1] // 2
    n_qk = (WIDTH + KV_WIDTH) // LANES
    proj_1 = jnp.dot(h, w_ref[:, :half], preferred_element_type=jnp.float32)
    proj_2 = jnp.dot(h, w_ref[:, half:], preferred_element_type=jnp.float32)

    cos_a = cos_a_ref[...]
    sin_a = sin_a_ref[...]
    for j0 in range(0, n_qk, MXU_COLS // LANES):
        nb = min(MXU_COLS // LANES, n_qk - j0)
        t = proj_1[:, j0 * LANES:(j0 + nb) * LANES]
        ssq = jnp.dot((t * t).astype(jnp.bfloat16), bd_ref[:nb * LANES, :nb * LANES],
                      preferred_element_type=jnp.float32)
        y = t * lax.rsqrt(ssq * (1.0 / HEAD_DIM) + EPS) * gqk_ref[:, j0 * LANES:(j0 + nb) * LANES]
        for j in range(j0, j0 + nb):
            r = _rope_block(y[:, (j - j0) * LANES:(j - j0 + 1) * LANES], cos_a, sin_a).astype(jnp.bfloat16)
            if j < n_qk - 1:
                qa_ref[0, :, j * LANES:(j + 1) * LANES] = r
            else:
                ka_ref[0] = r

    kb_ref[0] = _rope_block(proj_1[:, n_qk * LANES:], cos_bk_ref[...], sin_bk_ref[...]).astype(jnp.bfloat16)
    for j in range(WIDTH // LANES):
        t = proj_2[:, j * LANES:(j + 1) * LANES]
        qb_ref[0, :, j * LANES:(j + 1) * LANES] = _rope_block(t, cos_bq_ref[...], sin_bq_ref[...]).astype(jnp.bfloat16)

    for idx, (vt_ref, width) in enumerate(((vat_ref, KEY_CHUNK), (vbt_ref, Q_TILE))):
        o = WIDTH + idx * KV_WIDTH
        vt = proj_2[:, o:o + KV_WIDTH].T.astype(jnp.bfloat16)
        ones = jnp.ones((ONES_ROWS, width), jnp.bfloat16)
        for blk in range(tm // width):
            for kvh in range(N_KV):
                base = kvh * VT_ROWS
                vt_ref[0, blk, base:base + HEAD_DIM, :] = vt[kvh * HEAD_DIM:(kvh + 1) * HEAD_DIM,
                                                             blk * width:(blk + 1) * width]
                vt_ref[0, blk, base + HEAD_DIM:base + VT_ROWS, :] = ones


def _masked_queries(q_ref, kvh):
    tq = q_ref.shape[1]
    lane = lax.broadcasted_iota(jnp.int32, (tq, LANES), 1)
    keep = (lane < HEAD_DIM) if kvh == 0 else (lane >= HEAD_DIM)
    blocks = [jnp.where(keep, q_ref[0, :, j * LANES:(j + 1) * LANES], jnp.zeros((), jnp.bfloat16))
              for j in range(GROUP)]
    return jnp.concatenate(blocks, axis=0)


def _group_norm_store(head_outs, gain_ref, o_ref, group):
    full = jnp.concatenate(head_outs, axis=0)
    ms = jnp.mean(full * full, axis=0, keepdims=True)
    normed = (full * lax.rsqrt(ms + EPS) * gain_ref[...]).T.astype(o_ref.dtype)
    o_ref[0, :, group * WIDTH:(group + 1) * WIDTH] = normed


_NT = (((1,), (1,)), ((), ()))


def _attn_kernel(q_ref, qn_ref, k_ref, kn_ref, vt_ref, gain_a_ref,
                 qb_ref, kb0_ref, kb1_ref, kb2_ref, vb0_ref, vb1_ref, vb2_ref, bias_lo_ref, bias_hi_ref,
                 sink_ref, gain_b_ref, o_ref, s_ref, m_ref, sw_ref):
    n_chunks = k_ref.shape[1] // KEY_CHUNK
    tq = q_ref.shape[1]
    n_cols = GROUP * tq
    neg_inf = jnp.full((1, n_cols), -jnp.inf, jnp.float32)

    def scores(c, m, qx, keys_ref, buf):
        start = pl.multiple_of(c * KEY_CHUNK, KEY_CHUNK)
        s = lax.dot_general(keys_ref[0, pl.ds(start, KEY_CHUNK), :], qx, _NT,
                            preferred_element_type=jnp.float32)
        s_ref[buf, pl.ds(start, KEY_CHUNK), :] = s
        return jnp.maximum(m, jnp.max(s, axis=0, keepdims=True))

    def weighted(c, acc, m, buf):
        start = pl.multiple_of(c * KEY_CHUNK, KEY_CHUNK)
        p = jnp.exp2(s_ref[buf, pl.ds(start, KEY_CHUNK), :] - m).astype(jnp.bfloat16)
        vt = vt_ref[0, c, buf * VT_ROWS:(buf + 1) * VT_ROWS, :]
        return acc + jnp.dot(vt, p, preferred_element_type=jnp.float32)

    @pl.when((pl.program_id(0) == 0) & (pl.program_id(1) == 0))
    def _():
        qx = _masked_queries(q_ref, 0)
        m_ref[0] = lax.fori_loop(0, n_chunks, lambda c, m: scores(c, m, qx, k_ref, 0), neg_inf)

    window_plan = {}
    for slot, event in zip(WINDOW_SLOTS, (("scores", 0), ("scores", 1), ("weighted", 0), ("weighted", 1))):
        window_plan.setdefault(min(slot * n_chunks // 8, N_KV * n_chunks - 1), []).append(event)
    window_max = [None] * N_KV
    window_outs = [None] * N_HEADS

    head_outs = [None] * N_HEADS
    stages = ((1, q_ref, k_ref), (0, qn_ref, kn_ref))
    for kvh, (nxt, nq_ref, nk_ref) in enumerate(stages):
        qx = _masked_queries(nq_ref, nxt)
        m = m_ref[kvh]
        m_next = neg_inf
        acc = jnp.zeros((VT_ROWS, n_cols), jnp.float32)
        for c in range(n_chunks):
            m_next = scores(c, m_next, qx, nk_ref, nxt)
            acc = weighted(c, acc, m, kvh)
            for kind, wk in window_plan.get(kvh * n_chunks + c, ()):
                if kind == "scores":
                    window_max[wk] = _window_scores(qb_ref, (kb0_ref, kb1_ref, kb2_ref), bias_lo_ref, bias_hi_ref,
                                                    sink_ref, wk, sw_ref)
                else:
                    window_outs[wk * GROUP:(wk + 1) * GROUP] = _window_weighted(
                        (vb0_ref, vb1_ref, vb2_ref), sink_ref, wk, sw_ref, window_max[wk])
        m_ref[nxt] = m_next
        out = acc[:HEAD_DIM] / acc[HEAD_DIM:HEAD_DIM + 1]
        head_outs[kvh * GROUP:(kvh + 1) * GROUP] = [out[:, j * tq:(j + 1) * tq] for j in range(GROUP)]
    _group_norm_store(window_outs, gain_b_ref, o_ref, 1)
    _group_norm_store(head_outs, gain_a_ref, o_ref, 0)


def _window_scores(q_ref, k_refs, bias_lo_ref, bias_hi_ref, sink_ref, kvh, sw_ref):
    tq = q_ref.shape[1]
    qx = _masked_queries(q_ref, kvh)
    biases = (bias_lo_ref, None, bias_hi_ref)
    m = sink_ref[kvh:kvh + 1, :]
    for i, (k_ref, bias_ref) in enumerate(zip(k_refs, biases)):
        s = lax.dot_general(k_ref[0], qx, _NT, preferred_element_type=jnp.float32)
        if bias_ref is not None:
            s = s + bias_ref[0]
        sw_ref[kvh, i * tq:(i + 1) * tq, :] = s
        m = jnp.maximum(m, jnp.max(s, axis=0, keepdims=True))
    return m


def _window_weighted(v_refs, sink_ref, kvh, sw_ref, m):
    tq = v_refs[0].shape[3]
    p = jnp.exp2(sw_ref[kvh] - m).astype(jnp.bfloat16)
    rs = slice(kvh * VT_ROWS, (kvh + 1) * VT_ROWS)
    vt = jnp.concatenate([v_ref[0, 0, rs, :] for v_ref in v_refs], axis=1)
    acc = jnp.dot(vt, p, preferred_element_type=jnp.float32)
    den = acc[HEAD_DIM:HEAD_DIM + 1] + jnp.exp2(sink_ref[kvh:kvh + 1, :] - m)
    out = acc[:HEAD_DIM] / den
    return [out[:, j * tq:(j + 1) * tq] for j in range(GROUP)]


def _post_kernel(x_ref, mix_ref, wo_ref, gpost_ref, gffn_ref, wg_ref, wu_ref, wd_ref, gfpost_ref,
                 o_ref, acc_ref):
    mixed = jnp.dot(mix_ref[...], wo_ref[...], preferred_element_type=jnp.float32)
    x1 = x_ref[...] + _rms(mixed, gpost_ref[...])
    h = _rms(x1, gffn_ref[...]).astype(jnp.bfloat16)
    acc_ref[...] = jnp.zeros_like(acc_ref)
    for c in range(D_FF // FF_CHUNK):
        cols = slice(c * FF_CHUNK, (c + 1) * FF_CHUNK)
        g = jnp.dot(h, wg_ref[:, cols], preferred_element_type=jnp.float32)
        u = jnp.dot(h, wu_ref[:, cols], preferred_element_type=jnp.float32)
        a = (g / (1.0 + jnp.exp(-g)) * u).astype(jnp.bfloat16)
        acc_ref[...] += jnp.dot(a, wd_ref[cols, :], preferred_element_type=jnp.float32)
    o_ref[...] = x1 + _rms(acc_ref[...], gfpost_ref[...])


def _rope_tables(seq_len):
    rows = seq_len // GRID_W
    t = np.arange(seq_len, dtype=np.float32)
    row = np.repeat(np.arange(rows, dtype=np.float32), GRID_W)
    col = np.tile(np.arange(GRID_W, dtype=np.float32), rows)
    ax_pairs = HEAD_DIM // 4
    freq_ax = jnp.asarray(ROPE_THETA, jnp.float32) ** (-jnp.arange(ax_pairs, dtype=jnp.float32) / ax_pairs)
    ang_axial = jnp.concatenate([row[:, None] * freq_ax[None, :], col[:, None] * freq_ax[None, :]], axis=-1)
    n_pairs = HEAD_DIM // 2
    freq_1d = jnp.asarray(ROPE_THETA, jnp.float32) ** (-jnp.arange(n_pairs, dtype=jnp.float32) / n_pairs)
    ang_1d = t[:, None] * freq_1d[None, :]

    def tables(ang, scale):
        cos = jnp.cos(ang) * scale
        sin = jnp.sin(ang) * scale
        cos_head = jnp.concatenate([cos, cos], axis=-1)
        sin_head = jnp.concatenate([-sin, sin], axis=-1)
        return jnp.tile(cos_head, (1, LANES // HEAD_DIM)), jnp.tile(sin_head, (1, LANES // HEAD_DIM))

    return tables(ang_axial, 1.0), tables(ang_1d, SCORE_SCALE), tables(ang_1d, 1.0)


def _const_spec(shape):
    return pl.BlockSpec(shape, lambda *_: (0,) * len(shape), pipeline_mode=pl.Buffered(1))


def kernel(x, norm_mix_pre, w_in, q_norm_a, k_norm_a, sink_b, group_norm_a, group_norm_b, w_out,
           norm_mix_post, norm_ffn_pre, w_gate, w_up, w_down, norm_ffn_post):
    B, S, D = x.shape
    assert D == D_MODEL and S % ROW_TILE == 0 and S % GRID_W == 0
    f32, bf16 = jnp.float32, jnp.bfloat16
    n_row_tiles = S // ROW_TILE
    n_q_tiles = S // Q_TILE
    sub = ROW_TILE // Q_TILE

    w = w_in[0]
    q_cols = np.concatenate([np.arange(h * HEAD_DIM, (h + 1) * HEAD_DIM) for h in _Q_HEAD_ORDER])
    o_ka, o_va, o_qb, o_kb, o_vb = WIDTH, WIDTH + KV_WIDTH, WIDTH + 2 * KV_WIDTH, 2 * WIDTH + 2 * KV_WIDTH, 2 * WIDTH + 3 * KV_WIDTH
    col_idx = np.concatenate([q_cols, o_ka + np.arange(KV_WIDTH), o_kb + np.arange(KV_WIDTH), o_qb + q_cols,
                              o_va + np.arange(KV_WIDTH), o_vb + np.arange(KV_WIDTH)])
    w_p = w[:, col_idx].astype(bf16)
    gqk = jnp.concatenate([jnp.tile(q_norm_a[0] * SCORE_SCALE, N_HEADS), jnp.tile(k_norm_a[0], N_KV)])[None, :]
    head_of_lane = np.arange(MXU_COLS) // HEAD_DIM
    block_diag = jnp.asarray(head_of_lane[:, None] == head_of_lane[None, :], bf16)
    (cos_a, sin_a), (cos_bq, sin_bq), (cos_bk, sin_bk) = _rope_tables(S)

    table_spec = pl.BlockSpec((ROW_TILE, LANES), lambda i, b: (i, 0))
    q_out = pl.BlockSpec((1, ROW_TILE, WIDTH), lambda i, b: (b, i, 0))
    k_out = pl.BlockSpec((1, ROW_TILE, KV_WIDTH), lambda i, b: (b, i, 0))
    vat_out = pl.BlockSpec((1, ROW_TILE // KEY_CHUNK, N_KV * VT_ROWS, KEY_CHUNK), lambda i, b: (b, i, 0, 0))
    vbt_out = pl.BlockSpec((1, sub, N_KV * VT_ROWS, Q_TILE), lambda i, b: (b, i, 0, 0))
    qa, ka, vat, qb, kb, vbt = pl.pallas_call(
        _proj_kernel,
        grid=(n_row_tiles, B),
        in_specs=[pl.BlockSpec((1, ROW_TILE, D), lambda i, b: (b, i, 0)),
                  _const_spec((1, D)), _const_spec(w_p.shape), _const_spec(gqk.shape), _const_spec((MXU_COLS, MXU_COLS)),
                  table_spec, table_spec, table_spec, table_spec, table_spec, table_spec],
        out_specs=[q_out, k_out, vat_out, q_out, k_out, vbt_out],
        out_shape=[jax.ShapeDtypeStruct((B, S, WIDTH), bf16), jax.ShapeDtypeStruct((B, S, KV_WIDTH), bf16),
                   jax.ShapeDtypeStruct((B, S // KEY_CHUNK, N_KV * VT_ROWS, KEY_CHUNK), bf16),
                   jax.ShapeDtypeStruct((B, S, WIDTH), bf16), jax.ShapeDtypeStruct((B, S, KV_WIDTH), bf16),
                   jax.ShapeDtypeStruct((B, n_q_tiles, N_KV * VT_ROWS, Q_TILE), bf16)],
        compiler_params=pltpu.CompilerParams(dimension_semantics=("arbitrary", "arbitrary"),
                                             vmem_limit_bytes=VMEM_LIMIT),
        name="proj_rope",
    )(x, norm_mix_pre, w_p, gqk, block_diag, cos_a, sin_a, cos_bq, sin_bq, cos_bk, sin_bk)

    gain_a = jnp.broadcast_to(group_norm_a[0][:, None], (WIDTH, Q_TILE))
    gain_b = jnp.broadcast_to(group_norm_b[0][:, None], (WIDTH, Q_TILE))
    sink_cols = jnp.repeat(sink_b[0] * LOG2E, Q_TILE).reshape(N_KV, GROUP * Q_TILE)
    key_row = np.arange(Q_TILE)[:, None]
    q_col = np.arange(GROUP * Q_TILE)[None, :] % Q_TILE
    neg = np.float32(-np.inf)
    band_bias = jnp.asarray(np.stack([np.where(q_col <= key_row, np.float32(0), neg),
                                      np.where(key_row <= q_col, np.float32(0), neg),
                                      np.full((Q_TILE, GROUP * Q_TILE), neg)]).astype(np.float32))

    out_spec = pl.BlockSpec((1, Q_TILE, 2 * WIDTH), lambda b, t: (b, t, 0))
    q_spec = pl.BlockSpec((1, Q_TILE, WIDTH), lambda b, t: (b, t, 0))
    last_tile = B * n_q_tiles - 1

    def next_tile(b, t):
        nxt = jnp.minimum(b * n_q_tiles + t + 1, last_tile)
        return nxt // n_q_tiles, nxt % n_q_tiles

    def k_band_spec(d):
        return pl.BlockSpec((1, Q_TILE, KV_WIDTH), lambda b, t: (b, jnp.clip(t + d, 0, n_q_tiles - 1), 0))

    def v_band_spec(d):
        return pl.BlockSpec((1, 1, N_KV * VT_ROWS, Q_TILE),
                            lambda b, t: (b, jnp.clip(t + d, 0, n_q_tiles - 1), 0, 0))

    bias_block = (1, Q_TILE, GROUP * Q_TILE)
    mixer = pl.pallas_call(
        _attn_kernel,
        grid=(B, n_q_tiles),
        in_specs=[q_spec,
                  pl.BlockSpec((1, Q_TILE, WIDTH), lambda b, t: (*next_tile(b, t), 0)),
                  pl.BlockSpec((1, S, KV_WIDTH), lambda b, t: (b, 0, 0)),
                  pl.BlockSpec((1, S, KV_WIDTH), lambda b, t: (next_tile(b, t)[0], 0, 0)),
                  pl.BlockSpec((1, S // KEY_CHUNK, N_KV * VT_ROWS, KEY_CHUNK), lambda b, t: (b, 0, 0, 0)),
                  _const_spec((WIDTH, Q_TILE)),
                  q_spec, k_band_spec(-1), k_band_spec(0), k_band_spec(1),
                  v_band_spec(-1), v_band_spec(0), v_band_spec(1),
                  pl.BlockSpec(bias_block, lambda b, t: (jnp.where(t == 0, 2, 0), 0, 0)),
                  pl.BlockSpec(bias_block, lambda b, t: (jnp.where(t == n_q_tiles - 1, 2, 1), 0, 0)),
                  _const_spec(sink_cols.shape), _const_spec((WIDTH, Q_TILE))],
        out_specs=out_spec,
        out_shape=jax.ShapeDtypeStruct((B, S, 2 * WIDTH), bf16),
        scratch_shapes=[pltpu.VMEM((N_KV, S, GROUP * Q_TILE), f32),
                        pltpu.VMEM((N_KV, 1, GROUP * Q_TILE), f32),
                        pltpu.VMEM((N_KV, 3 * Q_TILE, GROUP * Q_TILE), f32)],
        compiler_params=pltpu.CompilerParams(dimension_semantics=("arbitrary", "arbitrary"),
                                             vmem_limit_bytes=VMEM_LIMIT),
        name="attention",
    )(qa, qa, ka, ka, vat, gain_a, qb, kb, kb, kb, vbt, vbt, vbt, band_bias, band_bias, sink_cols, gain_b)

    wg = w_gate[0].astype(bf16)
    wu = w_up[0].astype(bf16)
    wd = w_down[0].astype(bf16)
    wo = w_out[0].astype(bf16)
    row_spec = pl.BlockSpec((ROW_TILE, D), lambda i: (i, 0))
    mix_spec = pl.BlockSpec((ROW_TILE, 2 * WIDTH), lambda i: (i, 0))
    out = pl.pallas_call(
        _post_kernel,
        grid=(B * n_row_tiles,),
        in_specs=[row_spec, mix_spec, _const_spec(wo.shape), _const_spec((1, D)), _const_spec((1, D)),
                  _const_spec(wg.shape), _const_spec(wu.shape), _const_spec(wd.shape), _const_spec((1, D))],
        out_specs=row_spec,
        out_shape=jax.ShapeDtypeStruct((B * S, D), f32),
        scratch_shapes=[pltpu.VMEM((ROW_TILE, D), f32)],
        compiler_params=pltpu.CompilerParams(dimension_semantics=("arbitrary",), vmem_limit_bytes=VMEM_LIMIT),
        name="outproj_ffn",
    )(x.reshape(B * S, D), mixer.reshape(B * S, 2 * WIDTH), wo,
      norm_mix_post, norm_ffn_pre, wg, wu, wd, norm_ffn_post)
    return out.reshape(B, S, D)
```

```python
import math

import numpy as np
import jax
import jax.numpy as jnp
from jax import lax
from jax.experimental import pallas as pl
from jax.experimental.pallas import tpu as pltpu

D_MODEL = 1024
HEAD_DIM = 64
N_HEADS = 8
N_KV = 2
GROUP = N_HEADS // N_KV
WIDTH = N_HEADS * HEAD_DIM
KV_WIDTH = N_KV * HEAD_DIM
WINDOW = 128
GRID_W = 64
ROPE_THETA = 10000.0
EPS = 1e-6
D_FF = 2816
LOG2E = math.log2(math.e)
SCORE_SCALE = HEAD_DIM ** -0.5 * LOG2E

LANES = 128
MXU_COLS = 256
ONES_ROWS = 64
VT_ROWS = HEAD_DIM + ONES_ROWS
ROW_TILE = 512
Q_TILE = 128
KEY_CHUNK = 512
TILES_PER_STEP = 4
WINDOW_SLOTS = (0, 0, 6, 15)
FF_CHUNK = 256
VMEM_LIMIT = 56 * 1024 * 1024

_Q_HEAD_ORDER = tuple(h for j in range(GROUP) for h in (j, j + GROUP))


def _rms(x, gain):
    return x * lax.rsqrt(jnp.mean(x * x, axis=-1, keepdims=True) + EPS) * gain


def _rope_block(t, cos, sin_signed):
    lane = lax.broadcasted_iota(jnp.int32, t.shape, 1)
    first_half = (lane % HEAD_DIM) < (HEAD_DIM // 2)
    rot = jnp.where(first_half, pltpu.roll(t, LANES - HEAD_DIM // 2, 1), pltpu.roll(t, HEAD_DIM // 2, 1))
    return t * cos + rot * sin_signed


def _proj_kernel(x_ref, gpre_ref, w_ref, gqk_ref, bd_ref,
                 cos_a_ref, sin_a_ref, cos_bq_ref, sin_bq_ref, cos_bk_ref, sin_bk_ref,
                 qa_ref, ka_ref, vat_ref, qb_ref, kb_ref, vbt_ref):
    x = x_ref[0]
    h = _rms(x, gpre_ref[...]).astype(jnp.bfloat16)
    tm = x.shape[0]
    half = w_ref.shape[1] // 2
    n_qk = (WIDTH + KV_WIDTH) // LANES
    proj_1 = jnp.dot(h, w_ref[:, :half], preferred_element_type=jnp.float32)
    proj_2 = jnp.dot(h, w_ref[:, half:], preferred_element_type=jnp.float32)

    cos_a = cos_a_ref[...]
    sin_a = sin_a_ref[...]
    for j0 in range(0, n_qk, MXU_COLS // LANES):
        nb = min(MXU_COLS // LANES, n_qk - j0)
        t = proj_1[:, j0 * LANES:(j0 + nb) * LANES]
        ssq = jnp.dot((t * t).astype(jnp.bfloat16), bd_ref[:nb * LANES, :nb * LANES],
                      preferred_element_type=jnp.float32)
        y = t * lax.rsqrt(ssq * (1.0 / HEAD_DIM) + EPS) * gqk_ref[:, j0 * LANES:(j0 + nb) * LANES]
        for j in range(j0, j0 + nb):
            r = _rope_block(y[:, (j - j0) * LANES:(j - j0 + 1) * LANES], cos_a, sin_a).astype(jnp.bfloat16)
            if j < n_qk - 1:
                qa_ref[0, :, j * LANES:(j + 1) * LANES] = r
            else:
                ka_ref[0] = r

    kb_ref[0] = _rope_block(proj_1[:, n_qk * LANES:], cos_bk_ref[...], sin_bk_ref[...]).astype(jnp.bfloat16)
    for j in range(WIDTH // LANES):
        t = proj_2[:, j * LANES:(j + 1) * LANES]
        qb_ref[0, :, j * LANES:(j + 1) * LANES] = _rope_block(t, cos_bq_ref[...], sin_bq_ref[...]).astype(jnp.bfloat16)

    for idx, (vt_ref, width) in enumerate(((vat_ref, KEY_CHUNK), (vbt_ref, Q_TILE))):
        o = WIDTH + idx * KV_WIDTH
        vt = proj_2[:, o:o + KV_WIDTH].T.astype(jnp.bfloat16)
        ones = jnp.ones((ONES_ROWS, width), jnp.bfloat16)
        for blk in range(tm // width):
            for kvh in range(N_KV):
                base = kvh * VT_ROWS
                vt_ref[0, blk, base:base + HEAD_DIM, :] = vt[kvh * HEAD_DIM:(kvh + 1) * HEAD_DIM,
                                                             blk * width:(blk + 1) * width]
                vt_ref[0, blk, base + HEAD_DIM:base + VT_ROWS, :] = ones


def _masked_queries(q_ref, row0, kvh):
    lane = lax.broadcasted_iota(jnp.int32, (Q_TILE, LANES), 1)
    keep = (lane < HEAD_DIM) if kvh == 0 else (lane >= HEAD_DIM)
    blocks = [jnp.where(keep, q_ref[0, row0:row0 + Q_TILE, j * LANES:(j + 1) * LANES], jnp.zeros((), jnp.bfloat16))
              for j in range(GROUP)]
    return jnp.concatenate(blocks, axis=0)


def _group_norm_store(head_outs, gain_ref, o_ref, row0, group):
    full = jnp.concatenate(head_outs, axis=0)
    ms = jnp.mean(full * full, axis=0, keepdims=True)
    normed = (full * lax.rsqrt(ms + EPS) * gain_ref[...]).T.astype(o_ref.dtype)
    o_ref[0, row0:row0 + Q_TILE, group * WIDTH:(group + 1) * WIDTH] = normed


def _split_heads(out):
    return [out[:, j * Q_TILE:(j + 1) * Q_TILE] for j in range(GROUP)]


_NT = (((1,), (1,)), ((), ()))


def _attn_kernel(*refs):
    n_band = TILES_PER_STEP + 2
    q_ref, qn_ref, k_ref, kn_ref, vt_ref, gain_a_ref, qb_ref = refs[:7]
    kb_refs = refs[7:7 + n_band]
    vb_refs = refs[7 + n_band:7 + 2 * n_band]
    (bias_lo_edge_ref, bias_hi_edge_ref, bias_lo_ref, bias_hi_ref,
     sink_ref, gain_b_ref, o_ref, s_ref, m_ref, sw_ref) = refs[7 + 2 * n_band:]
    n_chunks = k_ref.shape[1] // KEY_CHUNK
    n_cols = GROUP * Q_TILE
    neg_inf = jnp.full((1, n_cols), -jnp.inf, jnp.float32)

    def scores(c, m, qx, keys_ref, buf):
        start = pl.multiple_of(c * KEY_CHUNK, KEY_CHUNK)
        s = lax.dot_general(keys_ref[0, pl.ds(start, KEY_CHUNK), :], qx, _NT,
                            preferred_element_type=jnp.float32)
        s_ref[buf, pl.ds(start, KEY_CHUNK), :] = s
        return jnp.maximum(m, jnp.max(s, axis=0, keepdims=True))

    def weighted(c, acc, m, buf):
        start = pl.multiple_of(c * KEY_CHUNK, KEY_CHUNK)
        p = jnp.exp2(s_ref[buf, pl.ds(start, KEY_CHUNK), :] - m).astype(jnp.bfloat16)
        vt = vt_ref[0, c, buf * VT_ROWS:(buf + 1) * VT_ROWS, :]
        return acc + jnp.dot(vt, p, preferred_element_type=jnp.float32)

    @pl.when((pl.program_id(0) == 0) & (pl.program_id(1) == 0))
    def _():
        qx = _masked_queries(q_ref, 0, 0)
        m_ref[0] = lax.fori_loop(0, n_chunks, lambda c, m: scores(c, m, qx, k_ref, 0), neg_inf)

    window_plan = {}
    for slot, event in zip(WINDOW_SLOTS, (("scores", 0), ("scores", 1), ("weighted", 0), ("weighted", 1))):
        window_plan.setdefault(min(slot * n_chunks // 8, N_KV * n_chunks - 1), []).append(event)

    for r in range(TILES_PER_STEP):
        row0 = r * Q_TILE
        last = r == TILES_PER_STEP - 1
        band_k = kb_refs[r:r + 3]
        band_v = vb_refs[r:r + 3]
        lo_ref = bias_lo_edge_ref if r == 0 else bias_lo_ref
        hi_ref = bias_hi_edge_ref if last else bias_hi_ref
        window_max = [None] * N_KV
        window_outs = [None] * N_HEADS
        head_outs = [None] * N_HEADS
        stages = ((1, q_ref, row0, k_ref),
                  (0, qn_ref, 0, kn_ref) if last else (0, q_ref, row0 + Q_TILE, k_ref))
        for kvh, (nxt, nq_ref, nrow0, nk_ref) in enumerate(stages):
            qx = _masked_queries(nq_ref, nrow0, nxt)
            m = m_ref[kvh]
            m_next = neg_inf
            acc = jnp.zeros((VT_ROWS, n_cols), jnp.float32)
            for c in range(n_chunks):
                for kind, wk in window_plan.get(kvh * n_chunks + c, ()):
                    if kind == "scores":
                        window_max[wk] = _window_scores(qb_ref, row0, band_k, lo_ref, hi_ref, sink_ref, wk, sw_ref)
                    else:
                        window_outs[wk * GROUP:(wk + 1) * GROUP] = _window_weighted(
                            band_v, sink_ref, wk, sw_ref, window_max[wk])
                m_next = scores(c, m_next, qx, nk_ref, nxt)
                acc = weighted(c, acc, m, kvh)
            m_ref[nxt] = m_next
            head_outs[kvh * GROUP:(kvh + 1) * GROUP] = _split_heads(acc[:HEAD_DIM] / acc[HEAD_DIM:HEAD_DIM + 1])
        _group_norm_store(window_outs, gain_b_ref, o_ref, row0, 1)
        _group_norm_store(head_outs, gain_a_ref, o_ref, row0, 0)


def _window_scores(q_ref, row0, k_refs, bias_lo_ref, bias_hi_ref, sink_ref, kvh, sw_ref):
    qx = _masked_queries(q_ref, row0, kvh)
    biases = (bias_lo_ref, None, bias_hi_ref)
    m = sink_ref[kvh:kvh + 1, :]
    for i, (k_ref, bias_ref) in enumerate(zip(k_refs, biases)):
        s = lax.dot_general(k_ref[0], qx, _NT, preferred_element_type=jnp.float32)
        if bias_ref is not None:
            s = s + bias_ref[0]
        sw_ref[kvh, i * Q_TILE:(i + 1) * Q_TILE, :] = s
        m = jnp.maximum(m, jnp.max(s, axis=0, keepdims=True))
    return m


def _window_weighted(v_refs, sink_ref, kvh, sw_ref, m):
    p = jnp.exp2(sw_ref[kvh] - m).astype(jnp.bfloat16)
    rs = slice(kvh * VT_ROWS, (kvh + 1) * VT_ROWS)
    vt = jnp.concatenate([v_ref[0, 0, rs, :] for v_ref in v_refs], axis=1)
    acc = jnp.dot(vt, p, preferred_element_type=jnp.float32)
    den = acc[HEAD_DIM:HEAD_DIM + 1] + jnp.exp2(sink_ref[kvh:kvh + 1, :] - m)
    return _split_heads(acc[:HEAD_DIM] / den)


def _post_kernel(x_ref, mix_ref, wo_ref, gpost_ref, gffn_ref, wg_ref, wu_ref, wd_ref, gfpost_ref,
                 o_ref, acc_ref):
    mixed = jnp.dot(mix_ref[...], wo_ref[...], preferred_element_type=jnp.float32)
    x1 = x_ref[...] + _rms(mixed, gpost_ref[...])
    h = _rms(x1, gffn_ref[...]).astype(jnp.bfloat16)
    acc_ref[...] = jnp.zeros_like(acc_ref)
    for c in range(D_FF // FF_CHUNK):
        cols = slice(c * FF_CHUNK, (c + 1) * FF_CHUNK)
        g = jnp.dot(h, wg_ref[:, cols], preferred_element_type=jnp.float32)
        u = jnp.dot(h, wu_ref[:, cols], preferred_element_type=jnp.float32)
        a = (g / (1.0 + jnp.exp(-g)) * u).astype(jnp.bfloat16)
        acc_ref[...] += jnp.dot(a, wd_ref[cols, :], preferred_element_type=jnp.float32)
    o_ref[...] = x1 + _rms(acc_ref[...], gfpost_ref[...])


def _rope_tables(seq_len):
    rows = seq_len // GRID_W
    t = np.arange(seq_len, dtype=np.float32)
    row = np.repeat(np.arange(rows, dtype=np.float32), GRID_W)
    col = np.tile(np.arange(GRID_W, dtype=np.float32), rows)
    ax_pairs = HEAD_DIM // 4
    freq_ax = jnp.asarray(ROPE_THETA, jnp.float32) ** (-jnp.arange(ax_pairs, dtype=jnp.float32) / ax_pairs)
    ang_axial = jnp.concatenate([row[:, None] * freq_ax[None, :], col[:, None] * freq_ax[None, :]], axis=-1)
    n_pairs = HEAD_DIM // 2
    freq_1d = jnp.asarray(ROPE_THETA, jnp.float32) ** (-jnp.arange(n_pairs, dtype=jnp.float32) / n_pairs)
    ang_1d = t[:, None] * freq_1d[None, :]

    def tables(ang, scale):
        cos = jnp.cos(ang) * scale
        sin = jnp.sin(ang) * scale
        cos_head = jnp.concatenate([cos, cos], axis=-1)
        sin_head = jnp.concatenate([-sin, sin], axis=-1)
        return jnp.tile(cos_head, (1, LANES // HEAD_DIM)), jnp.tile(sin_head, (1, LANES // HEAD_DIM))

    return tables(ang_axial, 1.0), tables(ang_1d, SCORE_SCALE), tables(ang_1d, 1.0)


def _const_spec(shape):
    return pl.BlockSpec(shape, lambda *_: (0,) * len(shape), pipeline_mode=pl.Buffered(1))


def kernel(x, norm_mix_pre, w_in, q_norm_a, k_norm_a, sink_b, group_norm_a, group_norm_b, w_out,
           norm_mix_post, norm_ffn_pre, w_gate, w_up, w_down, norm_ffn_post):
    B, S, D = x.shape
    assert D == D_MODEL and S % ROW_TILE == 0 and S % GRID_W == 0 and S % (TILES_PER_STEP * Q_TILE) == 0
    f32, bf16 = jnp.float32, jnp.bfloat16
    n_row_tiles = S // ROW_TILE
    n_q_tiles = S // Q_TILE
    sub = ROW_TILE // Q_TILE

    w = w_in[0]
    q_cols = np.concatenate([np.arange(h * HEAD_DIM, (h + 1) * HEAD_DIM) for h in _Q_HEAD_ORDER])
    o_ka, o_va, o_qb, o_kb, o_vb = WIDTH, WIDTH + KV_WIDTH, WIDTH + 2 * KV_WIDTH, 2 * WIDTH + 2 * KV_WIDTH, 2 * WIDTH + 3 * KV_WIDTH
    col_idx = np.concatenate([q_cols, o_ka + np.arange(KV_WIDTH), o_kb + np.arange(KV_WIDTH), o_qb + q_cols,
                              o_va + np.arange(KV_WIDTH), o_vb + np.arange(KV_WIDTH)])
    w_p = w[:, col_idx].astype(bf16)
    gqk = jnp.concatenate([jnp.tile(q_norm_a[0] * SCORE_SCALE, N_HEADS), jnp.tile(k_norm_a[0], N_KV)])[None, :]
    head_of_lane = np.arange(MXU_COLS) // HEAD_DIM
    block_diag = jnp.asarray(head_of_lane[:, None] == head_of_lane[None, :], bf16)
    (cos_a, sin_a), (cos_bq, sin_bq), (cos_bk, sin_bk) = _rope_tables(S)

    table_spec = pl.BlockSpec((ROW_TILE, LANES), lambda i, b: (i, 0))
    q_out = pl.BlockSpec((1, ROW_TILE, WIDTH), lambda i, b: (b, i, 0))
    k_out = pl.BlockSpec((1, ROW_TILE, KV_WIDTH), lambda i, b: (b, i, 0))
    vat_out = pl.BlockSpec((1, ROW_TILE // KEY_CHUNK, N_KV * VT_ROWS, KEY_CHUNK), lambda i, b: (b, i, 0, 0))
    vbt_out = pl.BlockSpec((1, sub, N_KV * VT_ROWS, Q_TILE), lambda i, b: (b, i, 0, 0))
    qa, ka, vat, qb, kb, vbt = pl.pallas_call(
        _proj_kernel,
        grid=(n_row_tiles, B),
        in_specs=[pl.BlockSpec((1, ROW_TILE, D), lambda i, b: (b, i, 0)),
                  _const_spec((1, D)), _const_spec(w_p.shape), _const_spec(gqk.shape), _const_spec((MXU_COLS, MXU_COLS)),
                  table_spec, table_spec, table_spec, table_spec, table_spec, table_spec],
        out_specs=[q_out, k_out, vat_out, q_out, k_out, vbt_out],
        out_shape=[jax.ShapeDtypeStruct((B, S, WIDTH), bf16), jax.ShapeDtypeStruct((B, S, KV_WIDTH), bf16),
                   jax.ShapeDtypeStruct((B, S // KEY_CHUNK, N_KV * VT_ROWS, KEY_CHUNK), bf16),
                   jax.ShapeDtypeStruct((B, S, WIDTH), bf16), jax.ShapeDtypeStruct((B, S, KV_WIDTH), bf16),
                   jax.ShapeDtypeStruct((B, n_q_tiles, N_KV * VT_ROWS, Q_TILE), bf16)],
        compiler_params=pltpu.CompilerParams(dimension_semantics=("arbitrary", "arbitrary"),
                                             vmem_limit_bytes=VMEM_LIMIT),
        name="proj_rope",
    )(x, norm_mix_pre, w_p, gqk, block_diag, cos_a, sin_a, cos_bq, sin_bq, cos_bk, sin_bk)

    gain_a = jnp.broadcast_to(group_norm_a[0][:, None], (WIDTH, Q_TILE))
    gain_b = jnp.broadcast_to(group_norm_b[0][:, None], (WIDTH, Q_TILE))
    sink_cols = jnp.repeat(sink_b[0] * LOG2E, Q_TILE).reshape(N_KV, GROUP * Q_TILE)
    key_row = np.arange(Q_TILE)[:, None]
    q_col = np.arange(GROUP * Q_TILE)[None, :] % Q_TILE
    neg = np.float32(-np.inf)
    band_bias = jnp.asarray(np.stack([np.where(q_col <= key_row, np.float32(0), neg),
                                      np.where(key_row <= q_col, np.float32(0), neg),
                                      np.full((Q_TILE, GROUP * Q_TILE), neg)]).astype(np.float32))

    step_rows = TILES_PER_STEP * Q_TILE
    n_steps = S // step_rows
    last_step = B * n_steps - 1
    out_spec = pl.BlockSpec((1, step_rows, 2 * WIDTH), lambda b, t: (b, t, 0))
    q_spec = pl.BlockSpec((1, step_rows, WIDTH), lambda b, t: (b, t, 0))

    def next_step(b, t):
        nxt = jnp.minimum(b * n_steps + t + 1, last_step)
        return nxt // n_steps, nxt % n_steps

    def next_q_index(b, t):
        nb, nt = next_step(b, t)
        return nb, nt * TILES_PER_STEP, 0

    def band_tile(t, j):
        return jnp.clip(t * TILES_PER_STEP - 1 + j, 0, n_q_tiles - 1)

    def k_band_spec(j):
        return pl.BlockSpec((1, Q_TILE, KV_WIDTH), lambda b, t: (b, band_tile(t, j), 0))

    def v_band_spec(j):
        return pl.BlockSpec((1, 1, N_KV * VT_ROWS, Q_TILE), lambda b, t: (b, band_tile(t, j), 0, 0))

    n_band = TILES_PER_STEP + 2
    bias_block = (1, Q_TILE, GROUP * Q_TILE)

    def fixed_bias_spec(i):
        return pl.BlockSpec(bias_block, lambda b, t: (i, 0, 0), pipeline_mode=pl.Buffered(1))

    mixer = pl.pallas_call(
        _attn_kernel,
        grid=(B, n_steps),
        in_specs=[q_spec,
                  pl.BlockSpec((1, Q_TILE, WIDTH), next_q_index),
                  pl.BlockSpec((1, S, KV_WIDTH), lambda b, t: (b, 0, 0)),
                  pl.BlockSpec((1, S, KV_WIDTH), lambda b, t: (next_step(b, t)[0], 0, 0)),
                  pl.BlockSpec((1, S // KEY_CHUNK, N_KV * VT_ROWS, KEY_CHUNK), lambda b, t: (b, 0, 0, 0)),
                  _const_spec((WIDTH, Q_TILE)),
                  q_spec, *[k_band_spec(j) for j in range(n_band)], *[v_band_spec(j) for j in range(n_band)],
                  pl.BlockSpec(bias_block, lambda b, t: (jnp.where(t == 0, 2, 0), 0, 0)),
                  pl.BlockSpec(bias_block, lambda b, t: (jnp.where(t == n_steps - 1, 2, 1), 0, 0)),
                  fixed_bias_spec(0), fixed_bias_spec(1),
                  _const_spec(sink_cols.shape), _const_spec((WIDTH, Q_TILE))],
        out_specs=out_spec,
        out_shape=jax.ShapeDtypeStruct((B, S, 2 * WIDTH), bf16),
        scratch_shapes=[pltpu.VMEM((N_KV, S, GROUP * Q_TILE), f32),
                        pltpu.VMEM((N_KV, 1, GROUP * Q_TILE), f32),
                        pltpu.VMEM((N_KV, 3 * Q_TILE, GROUP * Q_TILE), f32)],
        compiler_params=pltpu.CompilerParams(dimension_semantics=("arbitrary", "arbitrary"),
                                             vmem_limit_bytes=VMEM_LIMIT),
        name="attention",
    )(qa, qa, ka, ka, vat, gain_a, qb, *([kb] * n_band), *([vbt] * n_band),
      band_bias, band_bias, band_bias, band_bias, sink_cols, gain_b)

    wg = w_gate[0].astype(bf16)
    wu = w_up[0].astype(bf16)
    wd = w_down[0].astype(bf16)
    wo = w_out[0].astype(bf16)
    row_spec = pl.BlockSpec((ROW_TILE, D), lambda i: (i, 0))
    mix_spec = pl.BlockSpec((ROW_TILE, 2 * WIDTH), lambda i: (i, 0))
    out = pl.pallas_call(
        _post_kernel,
        grid=(B * n_row_tiles,),
        in_specs=[row_spec, mix_spec, _const_spec(wo.shape), _const_spec((1, D)), _const_spec((1, D)),
                  _const_spec(wg.shape), _const_spec(wu.shape), _const_spec(wd.shape), _const_spec((1, D))],
        out_specs=row_spec,
        out_shape=jax.ShapeDtypeStruct((B * S, D), f32),
        scratch_shapes=[pltpu.VMEM((ROW_TILE, D), f32)],
        compiler_params=pltpu.CompilerParams(dimension_semantics=("arbitrary",), vmem_limit_bytes=VMEM_LIMIT),
        name="outproj_ffn",
    )(x.reshape(B * S, D), mixer.reshape(B * S, 2 * WIDTH), wo,
      norm_mix_post, norm_ffn_pre, wg, wu, wd, norm_ffn_post)
    return out.reshape(B, S, D)
```

```python
import math

import numpy as np
import jax
import jax.numpy as jnp
from jax import lax
from jax.experimental import pallas as pl
from jax.experimental.pallas import tpu as pltpu

D_MODEL = 1024
HEAD_DIM = 64
N_HEADS = 8
N_KV = 2
GROUP = N_HEADS // N_KV
WIDTH = N_HEADS * HEAD_DIM
KV_WIDTH = N_KV * HEAD_DIM
WINDOW = 128
GRID_W = 64
ROPE_THETA = 10000.0
EPS = 1e-6
D_FF = 2816
LOG2E = math.log2(math.e)
SCORE_SCALE = HEAD_DIM ** -0.5 * LOG2E

LANES = 128
MXU_COLS = 256
ONES_ROWS = 64
VT_ROWS = HEAD_DIM + ONES_ROWS
ROW_TILE = 512
FFN_ROW_TILE = 1024
Q_TILE = 128
KEY_CHUNK = 512
TILES_PER_STEP = 4
WINDOW_SLOTS = (0, 0, 6, 15)
FF_CHUNK = 256
VMEM_LIMIT = 56 * 1024 * 1024

_PAIR_SLOT = (np.arange(LANES) // (HEAD_DIM // 2)) % 2
_PAIR_DIM = (np.arange(LANES) // HEAD_DIM) * (HEAD_DIM // 2) + np.arange(LANES) % (HEAD_DIM // 2)


def _rms(x, gain):
    return x * lax.rsqrt(jnp.mean(x * x, axis=-1, keepdims=True) + EPS) * gain


def _rope_block(t, cos, sin_signed):
    return t * cos + pltpu.roll(t, LANES // 2, 1) * sin_signed


def _proj_kernel(x_ref, gpre_ref, w_ref, gqk_ref, bd_ref,
                 cos_a_ref, sin_a_ref, cos_bq_ref, sin_bq_ref, cos_bk_ref, sin_bk_ref,
                 qa_ref, ka_ref, vat_ref, qb_ref, kb_ref, vbt_ref):
    x = x_ref[0]
    h = _rms(x, gpre_ref[...]).astype(jnp.bfloat16)
    tm = x.shape[0]
    half = w_ref.shape[1] // 2
    n_qk = (WIDTH + KV_WIDTH) // LANES
    proj_1 = jnp.dot(h, w_ref[:, :half], preferred_element_type=jnp.float32)
    proj_2 = jnp.dot(h, w_ref[:, half:], preferred_element_type=jnp.float32)

    cos_a = cos_a_ref[...]
    sin_a = sin_a_ref[...]
    for j0 in range(0, n_qk, MXU_COLS // LANES):
        nb = min(MXU_COLS // LANES, n_qk - j0)
        t = proj_1[:, j0 * LANES:(j0 + nb) * LANES]
        ssq = jnp.dot((t * t).astype(jnp.bfloat16), bd_ref[:nb * LANES, :nb * LANES],
                      preferred_element_type=jnp.float32)
        y = t * lax.rsqrt(ssq * (1.0 / HEAD_DIM) + EPS) * gqk_ref[:, j0 * LANES:(j0 + nb) * LANES]
        for j in range(j0, j0 + nb):
            r = _rope_block(y[:, (j - j0) * LANES:(j - j0 + 1) * LANES], cos_a, sin_a).astype(jnp.bfloat16)
            if j < n_qk - 1:
                qa_ref[0, :, j * LANES:(j + 1) * LANES] = r
            else:
                ka_ref[0] = r

    kb_ref[0] = _rope_block(proj_1[:, n_qk * LANES:], cos_bk_ref[...], sin_bk_ref[...]).astype(jnp.bfloat16)
    for j in range(WIDTH // LANES):
        t = proj_2[:, j * LANES:(j + 1) * LANES]
        qb_ref[0, :, j * LANES:(j + 1) * LANES] = _rope_block(t, cos_bq_ref[...], sin_bq_ref[...]).astype(jnp.bfloat16)

    for idx, (vt_ref, width) in enumerate(((vat_ref, KEY_CHUNK), (vbt_ref, Q_TILE))):
        o = WIDTH + idx * KV_WIDTH
        vt = proj_2[:, o:o + KV_WIDTH].T.astype(jnp.bfloat16)
        ones = jnp.ones((ONES_ROWS, width), jnp.bfloat16)
        for blk in range(tm // width):
            for kvh in range(N_KV):
                base = kvh * VT_ROWS
                vt_ref[0, blk, base:base + HEAD_DIM, :] = vt[kvh * HEAD_DIM:(kvh + 1) * HEAD_DIM,
                                                             blk * width:(blk + 1) * width]
                vt_ref[0, blk, base + HEAD_DIM:base + VT_ROWS, :] = ones


def _masked_queries(q_ref, row0, kvh):
    lane = lax.broadcasted_iota(jnp.int32, (Q_TILE, LANES), 1)
    keep = (lane // (HEAD_DIM // 2)) % 2 == kvh
    blocks = [jnp.where(keep, q_ref[0, row0:row0 + Q_TILE, j * LANES:(j + 1) * LANES], jnp.zeros((), jnp.bfloat16))
              for j in range(GROUP)]
    return jnp.concatenate(blocks, axis=0)


def _group_norm_store(head_outs, gain_ref, o_ref, row0, group):
    full = jnp.concatenate(head_outs, axis=0)
    ms = jnp.mean(full * full, axis=0, keepdims=True)
    normed = (full * lax.rsqrt(ms + EPS) * gain_ref[...]).T.astype(o_ref.dtype)
    o_ref[0, row0:row0 + Q_TILE, group * WIDTH:(group + 1) * WIDTH] = normed


def _split_heads(out):
    return [out[:, j * Q_TILE:(j + 1) * Q_TILE] for j in range(GROUP)]


_NT = (((1,), (1,)), ((), ()))


def _attn_kernel(*refs):
    n_band = TILES_PER_STEP + 2
    q_ref, qn_ref, k_ref, kn_ref, vt_ref, gain_a_ref, qb_ref = refs[:7]
    kb_refs = refs[7:7 + n_band]
    vb_refs = refs[7 + n_band:7 + 2 * n_band]
    (bias_lo_edge_ref, bias_hi_edge_ref, bias_lo_ref, bias_hi_ref,
     sink_ref, gain_b_ref, o_ref, s_ref, m_ref, sw_ref) = refs[7 + 2 * n_band:]
    n_chunks = k_ref.shape[1] // KEY_CHUNK
    n_cols = GROUP * Q_TILE
    neg_inf = jnp.full((1, n_cols), -jnp.inf, jnp.float32)

    def scores(c, m, qx, keys_ref, buf):
        start = pl.multiple_of(c * KEY_CHUNK, KEY_CHUNK)
        s = lax.dot_general(keys_ref[0, pl.ds(start, KEY_CHUNK), :], qx, _NT,
                            preferred_element_type=jnp.float32)
        s_ref[buf, pl.ds(start, KEY_CHUNK), :] = s
        return jnp.maximum(m, jnp.max(s, axis=0, keepdims=True))

    def weighted(c, acc, m, buf):
        start = pl.multiple_of(c * KEY_CHUNK, KEY_CHUNK)
        p = jnp.exp2(s_ref[buf, pl.ds(start, KEY_CHUNK), :] - m).astype(jnp.bfloat16)
        vt = vt_ref[0, c, buf * VT_ROWS:(buf + 1) * VT_ROWS, :]
        return acc + jnp.dot(vt, p, preferred_element_type=jnp.float32)

    @pl.when((pl.program_id(0) == 0) & (pl.program_id(1) == 0))
    def _():
        qx = _masked_queries(q_ref, 0, 0)
        m_ref[0] = lax.fori_loop(0, n_chunks, lambda c, m: scores(c, m, qx, k_ref, 0), neg_inf)

    window_plan = {}
    for slot, event in zip(WINDOW_SLOTS, (("scores", 0), ("scores", 1), ("weighted", 0), ("weighted", 1))):
        window_plan.setdefault(min(slot * n_chunks // 8, N_KV * n_chunks - 1), []).append(event)

    for r in range(TILES_PER_STEP):
        row0 = r * Q_TILE
        last = r == TILES_PER_STEP - 1
        band_k = kb_refs[r:r + 3]
        band_v = vb_refs[r:r + 3]
        lo_ref = bias_lo_edge_ref if r == 0 else bias_lo_ref
        hi_ref = bias_hi_edge_ref if last else bias_hi_ref
        window_max = [None] * N_KV
        window_outs = [None] * N_HEADS
        head_outs = [None] * N_HEADS
        stages = ((1, q_ref, row0, k_ref),
                  (0, qn_ref, 0, kn_ref) if last else (0, q_ref, row0 + Q_TILE, k_ref))
        for kvh, (nxt, nq_ref, nrow0, nk_ref) in enumerate(stages):
            qx = _masked_queries(nq_ref, nrow0, nxt)
            m = m_ref[kvh]
            m_next = neg_inf
            acc = jnp.zeros((VT_ROWS, n_cols), jnp.float32)
            for c in range(n_chunks):
                for kind, wk in window_plan.get(kvh * n_chunks + c, ()):
                    if kind == "scores":
                        window_max[wk] = _window_scores(qb_ref, row0, band_k, lo_ref, hi_ref, sink_ref, wk, sw_ref)
                    else:
                        window_outs[wk * GROUP:(wk + 1) * GROUP] = _window_weighted(
                            band_v, sink_ref, wk, sw_ref, window_max[wk])
                m_next = scores(c, m_next, qx, nk_ref, nxt)
                acc = weighted(c, acc, m, kvh)
            m_ref[nxt] = m_next
            head_outs[kvh * GROUP:(kvh + 1) * GROUP] = _split_heads(acc[:HEAD_DIM] / acc[HEAD_DIM:HEAD_DIM + 1])
        _group_norm_store(window_outs, gain_b_ref, o_ref, row0, 1)
        _group_norm_store(head_outs, gain_a_ref, o_ref, row0, 0)


def _window_scores(q_ref, row0, k_refs, bias_lo_ref, bias_hi_ref, sink_ref, kvh, sw_ref):
    qx = _masked_queries(q_ref, row0, kvh)
    k_band = jnp.concatenate([k_ref[0] for k_ref in k_refs], axis=0)
    s_band = lax.dot_general(k_band, qx, _NT, preferred_element_type=jnp.float32)
    biases = (bias_lo_ref, None, bias_hi_ref)
    m = sink_ref[kvh:kvh + 1, :]
    for i, bias_ref in enumerate(biases):
        s = s_band[i * Q_TILE:(i + 1) * Q_TILE]
        if bias_ref is not None:
            s = s + bias_ref[0]
        sw_ref[kvh, i * Q_TILE:(i + 1) * Q_TILE, :] = s
        m = jnp.maximum(m, jnp.max(s, axis=0, keepdims=True))
    return m


def _window_weighted(v_refs, sink_ref, kvh, sw_ref, m):
    p = jnp.exp2(sw_ref[kvh] - m).astype(jnp.bfloat16)
    rs = slice(kvh * VT_ROWS, (kvh + 1) * VT_ROWS)
    vt = jnp.concatenate([v_ref[0, 0, rs, :] for v_ref in v_refs], axis=1)
    acc = jnp.dot(vt, p, preferred_element_type=jnp.float32)
    den = acc[HEAD_DIM:HEAD_DIM + 1] + jnp.exp2(sink_ref[kvh:kvh + 1, :] - m)
    return _split_heads(acc[:HEAD_DIM] / den)


def _post_kernel(x_ref, mix_ref, wo_ref, gpost_ref, gffn_ref, wg_ref, wu_ref, wd_ref, gfpost_ref,
                 o_ref, acc_ref):
    mixed = jnp.dot(mix_ref[...], wo_ref[...], preferred_element_type=jnp.float32)
    x1 = x_ref[...] + _rms(mixed, gpost_ref[...])
    h = _rms(x1, gffn_ref[...]).astype(jnp.bfloat16)
    acc_ref[...] = jnp.zeros_like(acc_ref)
    for c in range(D_FF // FF_CHUNK):
        cols = slice(c * FF_CHUNK, (c + 1) * FF_CHUNK)
        g = jnp.dot(h, wg_ref[:, cols], preferred_element_type=jnp.float32)
        u = jnp.dot(h, wu_ref[:, cols], preferred_element_type=jnp.float32)
        a = (g / (1.0 + jnp.exp(-g)) * u).astype(jnp.bfloat16)
        acc_ref[...] += jnp.dot(a, wd_ref[cols, :], preferred_element_type=jnp.float32)
    o_ref[...] = x1 + _rms(acc_ref[...], gfpost_ref[...])


def _rope_tables(seq_len):
    rows = seq_len // GRID_W
    t = np.arange(seq_len, dtype=np.float32)
    row = np.repeat(np.arange(rows, dtype=np.float32), GRID_W)
    col = np.tile(np.arange(GRID_W, dtype=np.float32), rows)
    ax_pairs = HEAD_DIM // 4
    freq_ax = jnp.asarray(ROPE_THETA, jnp.float32) ** (-jnp.arange(ax_pairs, dtype=jnp.float32) / ax_pairs)
    ang_axial = jnp.concatenate([row[:, None] * freq_ax[None, :], col[:, None] * freq_ax[None, :]], axis=-1)
    n_pairs = HEAD_DIM // 2
    freq_1d = jnp.asarray(ROPE_THETA, jnp.float32) ** (-jnp.arange(n_pairs, dtype=jnp.float32) / n_pairs)
    ang_1d = t[:, None] * freq_1d[None, :]

    def tables(ang, scale):
        cos = jnp.cos(ang) * scale
        sin = jnp.sin(ang) * scale
        return jnp.tile(cos, (1, 4)), jnp.concatenate([-sin, -sin, sin, sin], axis=-1)

    return tables(ang_axial, 1.0), tables(ang_1d, SCORE_SCALE), tables(ang_1d, 1.0)


def _const_spec(shape):
    return pl.BlockSpec(shape, lambda *_: (0,) * len(shape), pipeline_mode=pl.Buffered(1))


def kernel(x, norm_mix_pre, w_in, q_norm_a, k_norm_a, sink_b, group_norm_a, group_norm_b, w_out,
           norm_mix_post, norm_ffn_pre, w_gate, w_up, w_down, norm_ffn_post):
    B, S, D = x.shape
    assert D == D_MODEL and S % ROW_TILE == 0 and S % GRID_W == 0 and S % (TILES_PER_STEP * Q_TILE) == 0
    f32, bf16 = jnp.float32, jnp.bfloat16
    n_row_tiles = S // ROW_TILE
    n_q_tiles = S // Q_TILE
    sub = ROW_TILE // Q_TILE

    w = w_in[0]
    q_cols = np.concatenate([(j + GROUP * _PAIR_SLOT) * HEAD_DIM + _PAIR_DIM for j in range(GROUP)])
    k_cols = _PAIR_SLOT * HEAD_DIM + _PAIR_DIM
    o_ka, o_va, o_qb, o_kb, o_vb = WIDTH, WIDTH + KV_WIDTH, WIDTH + 2 * KV_WIDTH, 2 * WIDTH + 2 * KV_WIDTH, 2 * WIDTH + 3 * KV_WIDTH
    col_idx = np.concatenate([q_cols, o_ka + k_cols, o_kb + k_cols, o_qb + q_cols,
                              o_va + np.arange(KV_WIDTH), o_vb + np.arange(KV_WIDTH)])
    w_p = w[:, col_idx].astype(bf16)
    gqk = jnp.concatenate([jnp.tile(q_norm_a[0][_PAIR_DIM] * SCORE_SCALE, GROUP), k_norm_a[0][_PAIR_DIM]])[None, :]
    head_of_lane = (np.arange(MXU_COLS) // LANES) * 2 + np.tile(_PAIR_SLOT, MXU_COLS // LANES)
    block_diag = jnp.asarray(head_of_lane[:, None] == head_of_lane[None, :], bf16)
    (cos_a, sin_a), (cos_bq, sin_bq), (cos_bk, sin_bk) = _rope_tables(S)

    table_spec = pl.BlockSpec((ROW_TILE, LANES), lambda i, b: (i, 0))
    q_out = pl.BlockSpec((1, ROW_TILE, WIDTH), lambda i, b: (b, i, 0))
    k_out = pl.BlockSpec((1, ROW_TILE, KV_WIDTH), lambda i, b: (b, i, 0))
    vat_out = pl.BlockSpec((1, ROW_TILE // KEY_CHUNK, N_KV * VT_ROWS, KEY_CHUNK), lambda i, b: (b, i, 0, 0))
    vbt_out = pl.BlockSpec((1, sub, N_KV * VT_ROWS, Q_TILE), lambda i, b: (b, i, 0, 0))
    qa, ka, vat, qb, kb, vbt = pl.pallas_call(
        _proj_kernel,
        grid=(n_row_tiles, B),
        in_specs=[pl.BlockSpec((1, ROW_TILE, D), lambda i, b: (b, i, 0)),
                  _const_spec((1, D)), _const_spec(w_p.shape), _const_spec(gqk.shape), _const_spec((MXU_COLS, MXU_COLS)),
                  table_spec, table_spec, table_spec, table_spec, table_spec, table_spec],
        out_specs=[q_out, k_out, vat_out, q_out, k_out, vbt_out],
        out_shape=[jax.ShapeDtypeStruct((B, S, WIDTH), bf16), jax.ShapeDtypeStruct((B, S, KV_WIDTH), bf16),
                   jax.ShapeDtypeStruct((B, S // KEY_CHUNK, N_KV * VT_ROWS, KEY_CHUNK), bf16),
                   jax.ShapeDtypeStruct((B, S, WIDTH), bf16), jax.ShapeDtypeStruct((B, S, KV_WIDTH), bf16),
                   jax.ShapeDtypeStruct((B, n_q_tiles, N_KV * VT_ROWS, Q_TILE), bf16)],
        compiler_params=pltpu.CompilerParams(dimension_semantics=("arbitrary", "arbitrary"),
                                             vmem_limit_bytes=VMEM_LIMIT),
        name="proj_rope",
    )(x, norm_mix_pre, w_p, gqk, block_diag, cos_a, sin_a, cos_bq, sin_bq, cos_bk, sin_bk)

    gain_a = jnp.broadcast_to(group_norm_a[0][:, None], (WIDTH, Q_TILE))
    gain_b = jnp.broadcast_to(group_norm_b[0][:, None], (WIDTH, Q_TILE))
    sink_cols = jnp.repeat(sink_b[0] * LOG2E, Q_TILE).reshape(N_KV, GROUP * Q_TILE)
    key_row = np.arange(Q_TILE)[:, None]
    q_col = np.arange(GROUP * Q_TILE)[None, :] % Q_TILE
    neg = np.float32(-np.inf)
    band_bias = jnp.asarray(np.stack([np.where(q_col <= key_row, np.float32(0), neg),
                                      np.where(key_row <= q_col, np.float32(0), neg),
                                      np.full((Q_TILE, GROUP * Q_TILE), neg)]).astype(np.float32))

    step_rows = TILES_PER_STEP * Q_TILE
    n_steps = S // step_rows
    last_step = B * n_steps - 1
    out_spec = pl.BlockSpec((1, step_rows, 2 * WIDTH), lambda b, t: (b, t, 0))
    q_spec = pl.BlockSpec((1, step_rows, WIDTH), lambda b, t: (b, t, 0))

    def next_step(b, t):
        nxt = jnp.minimum(b * n_steps + t + 1, last_step)
        return nxt // n_steps, nxt % n_steps

    def next_q_index(b, t):
        nb, nt = next_step(b, t)
        return nb, nt * TILES_PER_STEP, 0

    def band_tile(t, j):
        return jnp.clip(t * TILES_PER_STEP - 1 + j, 0, n_q_tiles - 1)

    def k_band_spec(j):
        return pl.BlockSpec((1, Q_TILE, KV_WIDTH), lambda b, t: (b, band_tile(t, j), 0))

    def v_band_spec(j):
        return pl.BlockSpec((1, 1, N_KV * VT_ROWS, Q_TILE), lambda b, t: (b, band_tile(t, j), 0, 0))

    n_band = TILES_PER_STEP + 2
    bias_block = (1, Q_TILE, GROUP * Q_TILE)

    def fixed_bias_spec(i):
        return pl.BlockSpec(bias_block, lambda b, t: (i, 0, 0), pipeline_mode=pl.Buffered(1))

    mixer = pl.pallas_call(
        _attn_kernel,
        grid=(B, n_steps),
        in_specs=[q_spec,
                  pl.BlockSpec((1, Q_TILE, WIDTH), next_q_index),
                  pl.BlockSpec((1, S, KV_WIDTH), lambda b, t: (b, 0, 0)),
                  pl.BlockSpec((1, S, KV_WIDTH), lambda b, t: (next_step(b, t)[0], 0, 0)),
                  pl.BlockSpec((1, S // KEY_CHUNK, N_KV * VT_ROWS, KEY_CHUNK), lambda b, t: (b, 0, 0, 0)),
                  _const_spec((WIDTH, Q_TILE)),
                  q_spec, *[k_band_spec(j) for j in range(n_band)], *[v_band_spec(j) for j in range(n_band)],
                  pl.BlockSpec(bias_block, lambda b, t: (jnp.where(t == 0, 2, 0), 0, 0)),
                  pl.BlockSpec(bias_block, lambda b, t: (jnp.where(t == n_steps - 1, 2, 1), 0, 0)),
                  fixed_bias_spec(0), fixed_bias_spec(1),
                  _const_spec(sink_cols.shape), _const_spec((WIDTH, Q_TILE))],
        out_specs=out_spec,
        out_shape=jax.ShapeDtypeStruct((B, S, 2 * WIDTH), bf16),
        scratch_shapes=[pltpu.VMEM((N_KV, S, GROUP * Q_TILE), f32),
                        pltpu.VMEM((N_KV, 1, GROUP * Q_TILE), f32),
                        pltpu.VMEM((N_KV, 3 * Q_TILE, GROUP * Q_TILE), f32)],
        compiler_params=pltpu.CompilerParams(dimension_semantics=("arbitrary", "arbitrary"),
                                             vmem_limit_bytes=VMEM_LIMIT),
        name="attention",
    )(qa, qa, ka, ka, vat, gain_a, qb, *([kb] * n_band), *([vbt] * n_band),
      band_bias, band_bias, band_bias, band_bias, sink_cols, gain_b)

    wg = w_gate[0].astype(bf16)
    wu = w_up[0].astype(bf16)
    wd = w_down[0].astype(bf16)
    wo = w_out[0].astype(bf16)
    row_spec = pl.BlockSpec((FFN_ROW_TILE, D), lambda i: (i, 0))
    mix_spec = pl.BlockSpec((FFN_ROW_TILE, 2 * WIDTH), lambda i: (i, 0))
    out = pl.pallas_call(
        _post_kernel,
        grid=(B * S // FFN_ROW_TILE,),
        in_specs=[row_spec, mix_spec, _const_spec(wo.shape), _const_spec((1, D)), _const_spec((1, D)),
                  _const_spec(wg.shape), _const_spec(wu.shape), _const_spec(wd.shape), _const_spec((1, D))],
        out_specs=row_spec,
        out_shape=jax.ShapeDtypeStruct((B * S, D), f32),
        scratch_shapes=[pltpu.VMEM((FFN_ROW_TILE, D), f32)],
        compiler_params=pltpu.CompilerParams(dimension_semantics=("arbitrary",), vmem_limit_bytes=VMEM_LIMIT),
        name="outproj_ffn",
    )(x.reshape(B * S, D), mixer.reshape(B * S, 2 * WIDTH), wo,
      norm_mix_post, norm_ffn_pre, wg, wu, wd, norm_ffn_post)
    return out.reshape(B, S, D)
```

```python
import math

import numpy as np
import jax
import jax.numpy as jnp
from jax import lax
from jax.experimental import pallas as pl
from jax.experimental.pallas import tpu as pltpu

D_MODEL = 1024
HEAD_DIM = 64
N_HEADS = 8
N_KV = 2
GROUP = N_HEADS // N_KV
WIDTH = N_HEADS * HEAD_DIM
KV_WIDTH = N_KV * HEAD_DIM
WINDOW = 128
GRID_W = 64
ROPE_THETA = 10000.0
EPS = 1e-6
D_FF = 2816
LOG2E = math.log2(math.e)
SCORE_SCALE = HEAD_DIM ** -0.5 * LOG2E

LANES = 128
MXU_COLS = 256
ONES_ROWS = 16
VT_ROWS = HEAD_DIM + ONES_ROWS
ROW_TILE = 512
FFN_ROW_TILE = 1024
Q_TILE = 128
KEY_CHUNK = 256
TILES_PER_STEP = 4
WINDOW_SLOTS = (0, 0, 6, 15)
FF_CHUNK = 256
VMEM_LIMIT = 56 * 1024 * 1024

_PAIR_SLOT = (np.arange(LANES) // (HEAD_DIM // 2)) % 2
_PAIR_DIM = (np.arange(LANES) // HEAD_DIM) * (HEAD_DIM // 2) + np.arange(LANES) % (HEAD_DIM // 2)


def _rms(x, gain):
    return x * lax.rsqrt(jnp.mean(x * x, axis=-1, keepdims=True) + EPS) * gain


def _rope_block(t, cos, sin_signed):
    return t * cos + pltpu.roll(t, LANES // 2, 1) * sin_signed


def _proj_kernel(x_ref, gpre_ref, w_ref, gqk_ref, bd_ref,
                 cos_a_ref, sin_a_ref, cos_bq_ref, sin_bq_ref, cos_bk_ref, sin_bk_ref,
                 qa_ref, ka_ref, vat_ref, qb_ref, kb_ref, vbt_ref):
    x = x_ref[0]
    h = _rms(x, gpre_ref[...]).astype(jnp.bfloat16)
    tm = x.shape[0]
    half = w_ref.shape[1] // 2
    n_qk = (WIDTH + KV_WIDTH) // LANES
    proj_1 = jnp.dot(h, w_ref[:, :half], preferred_element_type=jnp.float32)
    proj_2 = jnp.dot(h, w_ref[:, half:], preferred_element_type=jnp.float32)

    cos_a = cos_a_ref[...]
    sin_a = sin_a_ref[...]
    for j0 in range(0, n_qk, MXU_COLS // LANES):
        nb = min(MXU_COLS // LANES, n_qk - j0)
        t = proj_1[:, j0 * LANES:(j0 + nb) * LANES]
        ssq = jnp.dot((t * t).astype(jnp.bfloat16), bd_ref[:nb * LANES, :nb * LANES],
                      preferred_element_type=jnp.float32)
        y = t * lax.rsqrt(ssq * (1.0 / HEAD_DIM) + EPS) * gqk_ref[:, j0 * LANES:(j0 + nb) * LANES]
        for j in range(j0, j0 + nb):
            r = _rope_block(y[:, (j - j0) * LANES:(j - j0 + 1) * LANES], cos_a, sin_a).astype(jnp.bfloat16)
            if j < n_qk - 1:
                qa_ref[0, :, j * LANES:(j + 1) * LANES] = r
            else:
                ka_ref[0] = r

    kb_ref[0] = _rope_block(proj_1[:, n_qk * LANES:], cos_bk_ref[...], sin_bk_ref[...]).astype(jnp.bfloat16)
    for j in range(WIDTH // LANES):
        t = proj_2[:, j * LANES:(j + 1) * LANES]
        qb_ref[0, :, j * LANES:(j + 1) * LANES] = _rope_block(t, cos_bq_ref[...], sin_bq_ref[...]).astype(jnp.bfloat16)

    for idx, (vt_ref, width) in enumerate(((vat_ref, KEY_CHUNK), (vbt_ref, Q_TILE))):
        o = WIDTH + idx * KV_WIDTH
        vt = proj_2[:, o:o + KV_WIDTH].T.astype(jnp.bfloat16)
        ones = jnp.ones((ONES_ROWS, width), jnp.bfloat16)
        for blk in range(tm // width):
            for kvh in range(N_KV):
                base = kvh * VT_ROWS
                vt_ref[0, blk, base:base + HEAD_DIM, :] = vt[kvh * HEAD_DIM:(kvh + 1) * HEAD_DIM,
                                                             blk * width:(blk + 1) * width]
                vt_ref[0, blk, base + HEAD_DIM:base + VT_ROWS, :] = ones


def _masked_queries(q_ref, row0, kvh):
    lane = lax.broadcasted_iota(jnp.int32, (Q_TILE, LANES), 1)
    keep = (lane // (HEAD_DIM // 2)) % 2 == kvh
    blocks = [jnp.where(keep, q_ref[0, row0:row0 + Q_TILE, j * LANES:(j + 1) * LANES],
                        jnp.zeros((), jnp.bfloat16)).T for j in range(GROUP)]
    return jnp.concatenate(blocks, axis=1)


def _group_norm_store(head_outs, gain_ref, o_ref, row0, group):
    full = jnp.concatenate(head_outs, axis=0)
    ms = jnp.mean(full * full, axis=0, keepdims=True)
    normed = (full * lax.rsqrt(ms + EPS) * gain_ref[...]).T.astype(o_ref.dtype)
    o_ref[0, row0:row0 + Q_TILE, group * WIDTH:(group + 1) * WIDTH] = normed


def _split_heads(out):
    return [out[:, j * Q_TILE:(j + 1) * Q_TILE] for j in range(GROUP)]


def _attn_kernel(*refs):
    n_band = TILES_PER_STEP + 2
    q_ref, qn_ref, k_ref, kn_ref, vt_ref, gain_a_ref, qb_ref = refs[:7]
    kb_refs = refs[7:7 + n_band]
    vb_refs = refs[7 + n_band:7 + 2 * n_band]
    (bias_lo_edge_ref, bias_hi_edge_ref, bias_lo_ref, bias_hi_ref,
     sink_ref, gain_b_ref, o_ref, s_ref, m_ref, sw_ref) = refs[7 + 2 * n_band:]
    n_chunks = k_ref.shape[1] // KEY_CHUNK
    n_cols = GROUP * Q_TILE
    neg_inf = jnp.full((1, n_cols), -jnp.inf, jnp.float32)

    def scores(c, m, qx, keys_ref, buf):
        start = pl.multiple_of(c * KEY_CHUNK, KEY_CHUNK)
        s = jnp.dot(keys_ref[0, pl.ds(start, KEY_CHUNK), :], qx,
                    preferred_element_type=jnp.float32)
        s_ref[buf, pl.ds(start, KEY_CHUNK), :] = s
        return jnp.maximum(m, jnp.max(s, axis=0, keepdims=True))

    def weighted(c, acc, m, buf):
        start = pl.multiple_of(c * KEY_CHUNK, KEY_CHUNK)
        p = jnp.exp2(s_ref[buf, pl.ds(start, KEY_CHUNK), :] - m).astype(jnp.bfloat16)
        vt = vt_ref[0, c, buf * VT_ROWS:(buf + 1) * VT_ROWS, :]
        return acc + jnp.dot(vt, p, preferred_element_type=jnp.float32)

    @pl.when((pl.program_id(0) == 0) & (pl.program_id(1) == 0))
    def _():
        qx = _masked_queries(q_ref, 0, 0)
        m_ref[0] = lax.fori_loop(0, n_chunks, lambda c, m: scores(c, m, qx, k_ref, 0), neg_inf)

    window_plan = {}
    for slot, event in zip(WINDOW_SLOTS, (("scores", 0), ("scores", 1), ("weighted", 0), ("weighted", 1))):
        window_plan.setdefault(min(slot * n_chunks // 8, N_KV * n_chunks - 1), []).append(event)

    for r in range(TILES_PER_STEP):
        row0 = r * Q_TILE
        last = r == TILES_PER_STEP - 1
        band_k = kb_refs[r:r + 3]
        band_v = vb_refs[r:r + 3]
        lo_ref = bias_lo_edge_ref if r == 0 else bias_lo_ref
        hi_ref = bias_hi_edge_ref if last else bias_hi_ref
        window_max = [None] * N_KV
        window_outs = [None] * N_HEADS
        head_outs = [None] * N_HEADS
        stages = ((1, q_ref, row0, k_ref),
                  (0, qn_ref, 0, kn_ref) if last else (0, q_ref, row0 + Q_TILE, k_ref))
        for kvh, (nxt, nq_ref, nrow0, nk_ref) in enumerate(stages):
            qx = _masked_queries(nq_ref, nrow0, nxt)
            m = m_ref[kvh]
            m_next = neg_inf
            acc = jnp.zeros((VT_ROWS, n_cols), jnp.float32)
            for c in range(n_chunks):
                for kind, wk in window_plan.get(kvh * n_chunks + c, ()):
                    if kind == "scores":
                        window_max[wk] = _window_scores(qb_ref, row0, band_k, lo_ref, hi_ref, sink_ref, wk, sw_ref)
                    else:
                        window_outs[wk * GROUP:(wk + 1) * GROUP] = _window_weighted(
                            band_v, sink_ref, wk, sw_ref, window_max[wk])
                m_next = scores(c, m_next, qx, nk_ref, nxt)
                acc = weighted(c, acc, m, kvh)
            m_ref[nxt] = m_next
            head_outs[kvh * GROUP:(kvh + 1) * GROUP] = _split_heads(acc[:HEAD_DIM] / acc[HEAD_DIM:HEAD_DIM + 1])
        _group_norm_store(window_outs, gain_b_ref, o_ref, row0, 1)
        _group_norm_store(head_outs, gain_a_ref, o_ref, row0, 0)


def _window_scores(q_ref, row0, k_refs, bias_lo_ref, bias_hi_ref, sink_ref, kvh, sw_ref):
    qx = _masked_queries(q_ref, row0, kvh)
    k_band = jnp.concatenate([k_ref[0] for k_ref in k_refs], axis=0)
    s_band = jnp.dot(k_band, qx, preferred_element_type=jnp.float32)
    biases = (bias_lo_ref, None, bias_hi_ref)
    m = sink_ref[kvh:kvh + 1, :]
    for i, bias_ref in enumerate(biases):
        s = s_band[i * Q_TILE:(i + 1) * Q_TILE]
        if bias_ref is not None:
            s = s + bias_ref[0]
        sw_ref[kvh, i * Q_TILE:(i + 1) * Q_TILE, :] = s
        m = jnp.maximum(m, jnp.max(s, axis=0, keepdims=True))
    return m


def _window_weighted(v_refs, sink_ref, kvh, sw_ref, m):
    p = jnp.exp2(sw_ref[kvh] - m).astype(jnp.bfloat16)
    rs = slice(kvh * VT_ROWS, (kvh + 1) * VT_ROWS)
    vt = jnp.concatenate([v_ref[0, 0, rs, :] for v_ref in v_refs], axis=1)
    acc = jnp.dot(vt, p, preferred_element_type=jnp.float32)
    den = acc[HEAD_DIM:HEAD_DIM + 1] + jnp.exp2(sink_ref[kvh:kvh + 1, :] - m)
    return _split_heads(acc[:HEAD_DIM] / den)


def _post_kernel(x_ref, mix_ref, wo_ref, gpost_ref, gffn_ref, wg_ref, wu_ref, wd_ref, gfpost_ref,
                 o_ref, acc_ref):
    mixed = jnp.dot(mix_ref[...], wo_ref[...], preferred_element_type=jnp.float32)
    x1 = x_ref[...] + _rms(mixed, gpost_ref[...])
    h = _rms(x1, gffn_ref[...]).astype(jnp.bfloat16)
    acc_ref[...] = jnp.zeros_like(acc_ref)
    for c in range(D_FF // FF_CHUNK):
        cols = slice(c * FF_CHUNK, (c + 1) * FF_CHUNK)
        g = jnp.dot(h, wg_ref[:, cols], preferred_element_type=jnp.float32)
        u = jnp.dot(h, wu_ref[:, cols], preferred_element_type=jnp.float32)
        a = (g / (1.0 + jnp.exp(-g)) * u).astype(jnp.bfloat16)
        acc_ref[...] += jnp.dot(a, wd_ref[cols, :], preferred_element_type=jnp.float32)
    o_ref[...] = x1 + _rms(acc_ref[...], gfpost_ref[...])


def _rope_tables(seq_len):
    rows = seq_len // GRID_W
    t = np.arange(seq_len, dtype=np.float32)
    row = np.repeat(np.arange(rows, dtype=np.float32), GRID_W)
    col = np.tile(np.arange(GRID_W, dtype=np.float32), rows)
    ax_pairs = HEAD_DIM // 4
    freq_ax = jnp.asarray(ROPE_THETA, jnp.float32) ** (-jnp.arange(ax_pairs, dtype=jnp.float32) / ax_pairs)
    ang_axial = jnp.concatenate([row[:, None] * freq_ax[None, :], col[:, None] * freq_ax[None, :]], axis=-1)
    n_pairs = HEAD_DIM // 2
    freq_1d = jnp.asarray(ROPE_THETA, jnp.float32) ** (-jnp.arange(n_pairs, dtype=jnp.float32) / n_pairs)
    ang_1d = t[:, None] * freq_1d[None, :]

    def tables(ang, scale):
        cos = jnp.cos(ang) * scale
        sin = jnp.sin(ang) * scale
        return jnp.tile(cos, (1, 4)), jnp.concatenate([-sin, -sin, sin, sin], axis=-1)

    return tables(ang_axial, 1.0), tables(ang_1d, SCORE_SCALE), tables(ang_1d, 1.0)


def _const_spec(shape):
    return pl.BlockSpec(shape, lambda *_: (0,) * len(shape), pipeline_mode=pl.Buffered(1))


def kernel(x, norm_mix_pre, w_in, q_norm_a, k_norm_a, sink_b, group_norm_a, group_norm_b, w_out,
           norm_mix_post, norm_ffn_pre, w_gate, w_up, w_down, norm_ffn_post):
    B, S, D = x.shape
    assert D == D_MODEL and S % ROW_TILE == 0 and S % GRID_W == 0 and S % (TILES_PER_STEP * Q_TILE) == 0
    f32, bf16 = jnp.float32, jnp.bfloat16
    n_row_tiles = S // ROW_TILE
    n_q_tiles = S // Q_TILE
    sub = ROW_TILE // Q_TILE

    w = w_in[0]
    q_cols = np.concatenate([(j + GROUP * _PAIR_SLOT) * HEAD_DIM + _PAIR_DIM for j in range(GROUP)])
    k_cols = _PAIR_SLOT * HEAD_DIM + _PAIR_DIM
    o_ka, o_va, o_qb, o_kb, o_vb = WIDTH, WIDTH + KV_WIDTH, WIDTH + 2 * KV_WIDTH, 2 * WIDTH + 2 * KV_WIDTH, 2 * WIDTH + 3 * KV_WIDTH
    col_idx = np.concatenate([q_cols, o_ka + k_cols, o_kb + k_cols, o_qb + q_cols,
                              o_va + np.arange(KV_WIDTH), o_vb + np.arange(KV_WIDTH)])
    w_p = w[:, col_idx].astype(bf16)
    gqk = jnp.concatenate([jnp.tile(q_norm_a[0][_PAIR_DIM] * SCORE_SCALE, GROUP), k_norm_a[0][_PAIR_DIM]])[None, :]
    head_of_lane = (np.arange(MXU_COLS) // LANES) * 2 + np.tile(_PAIR_SLOT, MXU_COLS // LANES)
    block_diag = jnp.asarray(head_of_lane[:, None] == head_of_lane[None, :], bf16)
    (cos_a, sin_a), (cos_bq, sin_bq), (cos_bk, sin_bk) = _rope_tables(S)

    table_spec = pl.BlockSpec((ROW_TILE, LANES), lambda i, b: (i, 0))
    q_out = pl.BlockSpec((1, ROW_TILE, WIDTH), lambda i, b: (b, i, 0))
    k_out = pl.BlockSpec((1, ROW_TILE, KV_WIDTH), lambda i, b: (b, i, 0))
    vat_out = pl.BlockSpec((1, ROW_TILE // KEY_CHUNK, N_KV * VT_ROWS, KEY_CHUNK), lambda i, b: (b, i, 0, 0))
    vbt_out = pl.BlockSpec((1, sub, N_KV * VT_ROWS, Q_TILE), lambda i, b: (b, i, 0, 0))
    qa, ka, vat, qb, kb, vbt = pl.pallas_call(
        _proj_kernel,
        grid=(n_row_tiles, B),
        in_specs=[pl.BlockSpec((1, ROW_TILE, D), lambda i, b: (b, i, 0)),
                  _const_spec((1, D)), _const_spec(w_p.shape), _const_spec(gqk.shape), _const_spec((MXU_COLS, MXU_COLS)),
                  table_spec, table_spec, table_spec, table_spec, table_spec, table_spec],
        out_specs=[q_out, k_out, vat_out, q_out, k_out, vbt_out],
        out_shape=[jax.ShapeDtypeStruct((B, S, WIDTH), bf16), jax.ShapeDtypeStruct((B, S, KV_WIDTH), bf16),
                   jax.ShapeDtypeStruct((B, S // KEY_CHUNK, N_KV * VT_ROWS, KEY_CHUNK), bf16),
                   jax.ShapeDtypeStruct((B, S, WIDTH), bf16), jax.ShapeDtypeStruct((B, S, KV_WIDTH), bf16),
                   jax.ShapeDtypeStruct((B, n_q_tiles, N_KV * VT_ROWS, Q_TILE), bf16)],
        compiler_params=pltpu.CompilerParams(dimension_semantics=("arbitrary", "arbitrary"),
                                             vmem_limit_bytes=VMEM_LIMIT),
        name="proj_rope",
    )(x, norm_mix_pre, w_p, gqk, block_diag, cos_a, sin_a, cos_bq, sin_bq, cos_bk, sin_bk)

    gain_a = jnp.broadcast_to(group_norm_a[0][:, None], (WIDTH, Q_TILE))
    gain_b = jnp.broadcast_to(group_norm_b[0][:, None], (WIDTH, Q_TILE))
    sink_cols = jnp.repeat(sink_b[0] * LOG2E, Q_TILE).reshape(N_KV, GROUP * Q_TILE)
    key_row = np.arange(Q_TILE)[:, None]
    q_col = np.arange(GROUP * Q_TILE)[None, :] % Q_TILE
    neg = np.float32(-np.inf)
    band_bias = jnp.asarray(np.stack([np.where(q_col <= key_row, np.float32(0), neg),
                                      np.where(key_row <= q_col, np.float32(0), neg),
                                      np.full((Q_TILE, GROUP * Q_TILE), neg)]).astype(np.float32))

    step_rows = TILES_PER_STEP * Q_TILE
    n_steps = S // step_rows
    last_step = B * n_steps - 1
    out_spec = pl.BlockSpec((1, step_rows, 2 * WIDTH), lambda b, t: (b, t, 0))
    q_spec = pl.BlockSpec((1, step_rows, WIDTH), lambda b, t: (b, t, 0))

    def next_step(b, t):
        nxt = jnp.minimum(b * n_steps + t + 1, last_step)
        return nxt // n_steps, nxt % n_steps

    def next_q_index(b, t):
        nb, nt = next_step(b, t)
        return nb, nt * TILES_PER_STEP, 0

    def band_tile(t, j):
        return jnp.clip(t * TILES_PER_STEP - 1 + j, 0, n_q_tiles - 1)

    def k_band_spec(j):
        return pl.BlockSpec((1, Q_TILE, KV_WIDTH), lambda b, t: (b, band_tile(t, j), 0))

    def v_band_spec(j):
        return pl.BlockSpec((1, 1, N_KV * VT_ROWS, Q_TILE), lambda b, t: (b, band_tile(t, j), 0, 0))

    n_band = TILES_PER_STEP + 2
    bias_block = (1, Q_TILE, GROUP * Q_TILE)

    def fixed_bias_spec(i):
        return pl.BlockSpec(bias_block, lambda b, t: (i, 0, 0), pipeline_mode=pl.Buffered(1))

    mixer = pl.pallas_call(
        _attn_kernel,
        grid=(B, n_steps),
        in_specs=[q_spec,
                  pl.BlockSpec((1, Q_TILE, WIDTH), next_q_index),
                  pl.BlockSpec((1, S, KV_WIDTH), lambda b, t: (b, 0, 0)),
                  pl.BlockSpec((1, S, KV_WIDTH), lambda b, t: (next_step(b, t)[0], 0, 0)),
                  pl.BlockSpec((1, S // KEY_CHUNK, N_KV * VT_ROWS, KEY_CHUNK), lambda b, t: (b, 0, 0, 0)),
                  _const_spec((WIDTH, Q_TILE)),
                  q_spec, *[k_band_spec(j) for j in range(n_band)], *[v_band_spec(j) for j in range(n_band)],
                  pl.BlockSpec(bias_block, lambda b, t: (jnp.where(t == 0, 2, 0), 0, 0)),
                  pl.BlockSpec(bias_block, lambda b, t: (jnp.where(t == n_steps - 1, 2, 1), 0, 0)),
                  fixed_bias_spec(0), fixed_bias_spec(1),
                  _const_spec(sink_cols.shape), _const_spec((WIDTH, Q_TILE))],
        out_specs=out_spec,
        out_shape=jax.ShapeDtypeStruct((B, S, 2 * WIDTH), bf16),
        scratch_shapes=[pltpu.VMEM((N_KV, S, GROUP * Q_TILE), f32),
                        pltpu.VMEM((N_KV, 1, GROUP * Q_TILE), f32),
                        pltpu.VMEM((N_KV, 3 * Q_TILE, GROUP * Q_TILE), f32)],
        compiler_params=pltpu.CompilerParams(dimension_semantics=("arbitrary", "arbitrary"),
                                             vmem_limit_bytes=VMEM_LIMIT),
        name="attention",
    )(qa, qa, ka, ka, vat, gain_a, qb, *([kb] * n_band), *([vbt] * n_band),
      band_bias, band_bias, band_bias, band_bias, sink_cols, gain_b)

    wg = w_gate[0].astype(bf16)
    wu = w_up[0].astype(bf16)
    wd = w_down[0].astype(bf16)
    wo = w_out[0].astype(bf16)
    row_spec = pl.BlockSpec((FFN_ROW_TILE, D), lambda i: (i, 0))
    mix_spec = pl.BlockSpec((FFN_ROW_TILE, 2 * WIDTH), lambda i: (i, 0))
    out = pl.pallas_call(
        _post_kernel,
        grid=(B * S // FFN_ROW_TILE,),
        in_specs=[row_spec, mix_spec, _const_spec(wo.shape), _const_spec((1, D)), _const_spec((1, D)),
                  _const_spec(wg.shape), _const_spec(wu.shape), _const_spec(wd.shape), _const_spec((1, D))],
        out_specs=row_spec,
        out_shape=jax.ShapeDtypeStruct((B * S, D), f32),
        scratch_shapes=[pltpu.VMEM((FFN_ROW_TILE, D), f32)],
        compiler_params=pltpu.CompilerParams(dimension_semantics=("arbitrary",), vmem_limit_bytes=VMEM_LIMIT),
        name="outproj_ffn",
    )(x.reshape(B * S, D), mixer.reshape(B * S, 2 * WIDTH), wo,
      norm_mix_post, norm_ffn_pre, wg, wu, wd, norm_ffn_post)
    return out.reshape(B, S, D)
```

```python
import math

import numpy as np
import jax
import jax.numpy as jnp
from jax import lax
from jax.experimental import pallas as pl
from jax.experimental.pallas import tpu as pltpu

D_MODEL = 1024
HEAD_DIM = 64
N_HEADS = 8
N_KV = 2
GROUP = N_HEADS // N_KV
WIDTH = N_HEADS * HEAD_DIM
KV_WIDTH = N_KV * HEAD_DIM
WINDOW = 128
GRID_W = 64
ROPE_THETA = 10000.0
EPS = 1e-6
D_FF = 2816
LOG2E = math.log2(math.e)
SCORE_SCALE = HEAD_DIM ** -0.5 * LOG2E

LANES = 128
MXU_COLS = 256
ONES_ROWS = 16
VT_ROWS = HEAD_DIM + ONES_ROWS
ROW_TILE = 512
FFN_ROW_TILE = 1024
Q_TILE = 128
KEY_CHUNK = 256
TILES_PER_STEP = 4
EARLY_CHUNKS = 2
EARLY_LAG = 2
WINDOW_SLOTS = (0, 0, 6, 15)
FF_CHUNK = 256
VMEM_LIMIT = 56 * 1024 * 1024

_PAIR_SLOT = (np.arange(LANES) // (HEAD_DIM // 2)) % 2
_PAIR_DIM = (np.arange(LANES) // HEAD_DIM) * (HEAD_DIM // 2) + np.arange(LANES) % (HEAD_DIM // 2)


def _rms(x, gain):
    return x * lax.rsqrt(jnp.mean(x * x, axis=-1, keepdims=True) + EPS) * gain


def _rope_block(t, cos, sin_signed):
    return t * cos + pltpu.roll(t, LANES // 2, 1) * sin_signed


def _proj_kernel(x_ref, gpre_ref, w_ref, gqk_ref, bd_ref,
                 cos_a_ref, sin_a_ref, cos_bq_ref, sin_bq_ref, cos_bk_ref, sin_bk_ref,
                 qa_ref, ka_ref, vat_ref, qb_ref, kb_ref, vbt_ref):
    x = x_ref[0]
    h = _rms(x, gpre_ref[...]).astype(jnp.bfloat16)
    tm = x.shape[0]
    half = w_ref.shape[1] // 2
    n_qk = (WIDTH + KV_WIDTH) // LANES
    proj_1 = jnp.dot(h, w_ref[:, :half], preferred_element_type=jnp.float32)
    proj_2 = jnp.dot(h, w_ref[:, half:], preferred_element_type=jnp.float32)

    cos_a = cos_a_ref[...]
    sin_a = sin_a_ref[...]
    for j0 in range(0, n_qk, MXU_COLS // LANES):
        nb = min(MXU_COLS // LANES, n_qk - j0)
        t = proj_1[:, j0 * LANES:(j0 + nb) * LANES]
        ssq = jnp.dot((t * t).astype(jnp.bfloat16), bd_ref[:nb * LANES, :nb * LANES],
                      preferred_element_type=jnp.float32)
        y = t * lax.rsqrt(ssq * (1.0 / HEAD_DIM) + EPS) * gqk_ref[:, j0 * LANES:(j0 + nb) * LANES]
        for j in range(j0, j0 + nb):
            r = _rope_block(y[:, (j - j0) * LANES:(j - j0 + 1) * LANES], cos_a, sin_a).astype(jnp.bfloat16)
            if j < n_qk - 1:
                qa_ref[0, :, j * LANES:(j + 1) * LANES] = r
            else:
                ka_ref[0] = r

    kb_ref[0] = _rope_block(proj_1[:, n_qk * LANES:], cos_bk_ref[...], sin_bk_ref[...]).astype(jnp.bfloat16)
    for j in range(WIDTH // LANES):
        t = proj_2[:, j * LANES:(j + 1) * LANES]
        qb_ref[0, :, j * LANES:(j + 1) * LANES] = _rope_block(t, cos_bq_ref[...], sin_bq_ref[...]).astype(jnp.bfloat16)

    for idx, (vt_ref, width) in enumerate(((vat_ref, KEY_CHUNK), (vbt_ref, Q_TILE))):
        o = WIDTH + idx * KV_WIDTH
        vt = proj_2[:, o:o + KV_WIDTH].T.astype(jnp.bfloat16)
        ones = jnp.ones((ONES_ROWS, width), jnp.bfloat16)
        for blk in range(tm // width):
            for kvh in range(N_KV):
                base = kvh * VT_ROWS
                vt_ref[0, blk, base:base + HEAD_DIM, :] = vt[kvh * HEAD_DIM:(kvh + 1) * HEAD_DIM,
                                                             blk * width:(blk + 1) * width]
                vt_ref[0, blk, base + HEAD_DIM:base + VT_ROWS, :] = ones


def _masked_queries(q_ref, row0, kvh):
    lane = lax.broadcasted_iota(jnp.int32, (Q_TILE, LANES), 1)
    keep = (lane // (HEAD_DIM // 2)) % 2 == kvh
    blocks = [jnp.where(keep, q_ref[0, row0:row0 + Q_TILE, j * LANES:(j + 1) * LANES],
                        jnp.zeros((), jnp.bfloat16)).T for j in range(GROUP)]
    return jnp.concatenate(blocks, axis=1)


def _group_norm_store(head_outs, gain_ref, o_ref, row0, group):
    full = jnp.concatenate(head_outs, axis=0)
    ms = jnp.mean(full * full, axis=0, keepdims=True)
    normed = (full * lax.rsqrt(ms + EPS) * gain_ref[...]).T.astype(o_ref.dtype)
    o_ref[0, row0:row0 + Q_TILE, group * WIDTH:(group + 1) * WIDTH] = normed


def _split_heads(out):
    return [out[:, j * Q_TILE:(j + 1) * Q_TILE] for j in range(GROUP)]


def _attn_kernel(*refs):
    n_band = TILES_PER_STEP + 2
    q_ref, qn_ref, k_ref, kn_ref, vt_ref, gain_a_ref, qb_ref = refs[:7]
    kb_refs = refs[7:7 + n_band]
    vb_refs = refs[7 + n_band:7 + 2 * n_band]
    (bias_lo_edge_ref, bias_hi_edge_ref, bias_lo_ref, bias_hi_ref,
     sink_ref, gain_b_ref, o_ref, s_ref, m_ref, sw_ref) = refs[7 + 2 * n_band:]
    n_chunks = k_ref.shape[1] // KEY_CHUNK
    n_cols = GROUP * Q_TILE
    neg_inf = jnp.full((1, n_cols), -jnp.inf, jnp.float32)

    def scores(c, m, qx, keys_ref, buf):
        start = pl.multiple_of(c * KEY_CHUNK, KEY_CHUNK)
        s = jnp.dot(keys_ref[0, pl.ds(start, KEY_CHUNK), :], qx,
                    preferred_element_type=jnp.float32)
        s_ref[buf, pl.ds(start, KEY_CHUNK), :] = s
        return jnp.maximum(m, jnp.max(s, axis=0, keepdims=True))

    def weighted(c, acc, m, buf):
        start = pl.multiple_of(c * KEY_CHUNK, KEY_CHUNK)
        p = jnp.exp2(s_ref[buf, pl.ds(start, KEY_CHUNK), :] - m).astype(jnp.bfloat16)
        vt = vt_ref[0, c, buf * VT_ROWS:(buf + 1) * VT_ROWS, :]
        return acc + jnp.dot(vt, p, preferred_element_type=jnp.float32)

    early_after = n_chunks - 1 - EARLY_LAG
    assert EARLY_CHUNKS - 1 <= early_after

    @pl.when((pl.program_id(0) == 0) & (pl.program_id(1) == 0))
    def _():
        qx = _masked_queries(q_ref, 0, 0)
        m_early = lax.fori_loop(0, early_after + 1, lambda c, m: scores(c, m, qx, k_ref, 0), neg_inf)
        m_ref[0] = m_early
        m_ref[1] = lax.fori_loop(early_after + 1, n_chunks, lambda c, m: scores(c, m, qx, k_ref, 0), m_early)

    window_plan = {}
    for slot, event in zip(WINDOW_SLOTS, (("scores", 0), ("scores", 1), ("weighted", 0), ("weighted", 1))):
        window_plan.setdefault(min(slot * n_chunks // 8, N_KV * n_chunks - 1), []).append(event)

    for r in range(TILES_PER_STEP):
        row0 = r * Q_TILE
        last = r == TILES_PER_STEP - 1
        band_k = kb_refs[r:r + 3]
        band_v = vb_refs[r:r + 3]
        lo_ref = bias_lo_edge_ref if r == 0 else bias_lo_ref
        hi_ref = bias_hi_edge_ref if last else bias_hi_ref
        window_max = [None] * N_KV
        window_outs = [None] * N_HEADS
        head_outs = [None] * N_HEADS
        stages = ((1, q_ref, row0, k_ref),
                  (0, qn_ref, 0, kn_ref) if last else (0, q_ref, row0 + Q_TILE, k_ref))
        for kvh, (nxt, nq_ref, nrow0, nk_ref) in enumerate(stages):
            qx = _masked_queries(nq_ref, nrow0, nxt)
            m_early = m_ref[2 * kvh]
            m = m_ref[2 * kvh + 1]
            m_next = neg_inf
            acc_early = jnp.zeros((VT_ROWS, n_cols), jnp.float32)
            acc = jnp.zeros((VT_ROWS, n_cols), jnp.float32)
            for c in range(n_chunks):
                for kind, wk in window_plan.get(kvh * n_chunks + c, ()):
                    if kind == "scores":
                        window_max[wk] = _window_scores(qb_ref, row0, band_k, lo_ref, hi_ref, sink_ref, wk, sw_ref)
                    else:
                        window_outs[wk * GROUP:(wk + 1) * GROUP] = _window_weighted(
                            band_v, sink_ref, wk, sw_ref, window_max[wk])
                m_next = scores(c, m_next, qx, nk_ref, nxt)
                if c == early_after:
                    m_ref[2 * nxt] = m_next
                if c < EARLY_CHUNKS:
                    acc_early = weighted(c, acc_early, m_early, kvh)
                else:
                    acc = weighted(c, acc, m, kvh)
            m_ref[2 * nxt + 1] = m_next
            acc = acc + acc_early * jnp.exp2(m_early - m)
            head_outs[kvh * GROUP:(kvh + 1) * GROUP] = _split_heads(acc[:HEAD_DIM] / acc[HEAD_DIM:HEAD_DIM + 1])
        _group_norm_store(window_outs, gain_b_ref, o_ref, row0, 1)
        _group_norm_store(head_outs, gain_a_ref, o_ref, row0, 0)


def _window_scores(q_ref, row0, k_refs, bias_lo_ref, bias_hi_ref, sink_ref, kvh, sw_ref):
    qx = _masked_queries(q_ref, row0, kvh)
    k_band = jnp.concatenate([k_ref[0] for k_ref in k_refs], axis=0)
    s_band = jnp.dot(k_band, qx, preferred_element_type=jnp.float32)
    biases = (bias_lo_ref, None, bias_hi_ref)
    m = sink_ref[kvh:kvh + 1, :]
    for i, bias_ref in enumerate(biases):
        s = s_band[i * Q_TILE:(i + 1) * Q_TILE]
        if bias_ref is not None:
            s = s + bias_ref[0]
        sw_ref[kvh, i * Q_TILE:(i + 1) * Q_TILE, :] = s
        m = jnp.maximum(m, jnp.max(s, axis=0, keepdims=True))
    return m


def _window_weighted(v_refs, sink_ref, kvh, sw_ref, m):
    p = jnp.exp2(sw_ref[kvh] - m).astype(jnp.bfloat16)
    rs = slice(kvh * VT_ROWS, (kvh + 1) * VT_ROWS)
    vt = jnp.concatenate([v_ref[0, 0, rs, :] for v_ref in v_refs], axis=1)
    acc = jnp.dot(vt, p, preferred_element_type=jnp.float32)
    den = acc[HEAD_DIM:HEAD_DIM + 1] + jnp.exp2(sink_ref[kvh:kvh + 1, :] - m)
    return _split_heads(acc[:HEAD_DIM] / den)


def _post_kernel(x_ref, mix_ref, wo_ref, gpost_ref, gffn_ref, wg_ref, wu_ref, wd_ref, gfpost_ref,
                 o_ref, acc_ref):
    mixed = jnp.dot(mix_ref[...], wo_ref[...], preferred_element_type=jnp.float32)
    x1 = x_ref[...] + _rms(mixed, gpost_ref[...])
    h = _rms(x1, gffn_ref[...]).astype(jnp.bfloat16)
    acc_ref[...] = jnp.zeros_like(acc_ref)
    for c in range(D_FF // FF_CHUNK):
        cols = slice(c * FF_CHUNK, (c + 1) * FF_CHUNK)
        g = jnp.dot(h, wg_ref[:, cols], preferred_element_type=jnp.float32)
        u = jnp.dot(h, wu_ref[:, cols], preferred_element_type=jnp.float32)
        a = (g / (1.0 + jnp.exp(-g)) * u).astype(jnp.bfloat16)
        acc_ref[...] += jnp.dot(a, wd_ref[cols, :], preferred_element_type=jnp.float32)
    o_ref[...] = x1 + _rms(acc_ref[...], gfpost_ref[...])


def _rope_tables(seq_len):
    rows = seq_len // GRID_W
    t = np.arange(seq_len, dtype=np.float32)
    row = np.repeat(np.arange(rows, dtype=np.float32), GRID_W)
    col = np.tile(np.arange(GRID_W, dtype=np.float32), rows)
    ax_pairs = HEAD_DIM // 4
    freq_ax = jnp.asarray(ROPE_THETA, jnp.float32) ** (-jnp.arange(ax_pairs, dtype=jnp.float32) / ax_pairs)
    ang_axial = jnp.concatenate([row[:, None] * freq_ax[None, :], col[:, None] * freq_ax[None, :]], axis=-1)
    n_pairs = HEAD_DIM // 2
    freq_1d = jnp.asarray(ROPE_THETA, jnp.float32) ** (-jnp.arange(n_pairs, dtype=jnp.float32) / n_pairs)
    ang_1d = t[:, None] * freq_1d[None, :]

    def tables(ang, scale):
        cos = jnp.cos(ang) * scale
        sin = jnp.sin(ang) * scale
        return jnp.tile(cos, (1, 4)), jnp.concatenate([-sin, -sin, sin, sin], axis=-1)

    return tables(ang_axial, 1.0), tables(ang_1d, SCORE_SCALE), tables(ang_1d, 1.0)


def _const_spec(shape):
    return pl.BlockSpec(shape, lambda *_: (0,) * len(shape), pipeline_mode=pl.Buffered(1))


def kernel(x, norm_mix_pre, w_in, q_norm_a, k_norm_a, sink_b, group_norm_a, group_norm_b, w_out,
           norm_mix_post, norm_ffn_pre, w_gate, w_up, w_down, norm_ffn_post):
    B, S, D = x.shape
    assert D == D_MODEL and S % ROW_TILE == 0 and S % GRID_W == 0 and S % (TILES_PER_STEP * Q_TILE) == 0
    f32, bf16 = jnp.float32, jnp.bfloat16
    n_row_tiles = S // ROW_TILE
    n_q_tiles = S // Q_TILE
    sub = ROW_TILE // Q_TILE

    w = w_in[0]
    q_cols = np.concatenate([(j + GROUP * _PAIR_SLOT) * HEAD_DIM + _PAIR_DIM for j in range(GROUP)])
    k_cols = _PAIR_SLOT * HEAD_DIM + _PAIR_DIM
    o_ka, o_va, o_qb, o_kb, o_vb = WIDTH, WIDTH + KV_WIDTH, WIDTH + 2 * KV_WIDTH, 2 * WIDTH + 2 * KV_WIDTH, 2 * WIDTH + 3 * KV_WIDTH
    col_idx = np.concatenate([q_cols, o_ka + k_cols, o_kb + k_cols, o_qb + q_cols,
                              o_va + np.arange(KV_WIDTH), o_vb + np.arange(KV_WIDTH)])
    w_p = w[:, col_idx].astype(bf16)
    gqk = jnp.concatenate([jnp.tile(q_norm_a[0][_PAIR_DIM] * SCORE_SCALE, GROUP), k_norm_a[0][_PAIR_DIM]])[None, :]
    head_of_lane = (np.arange(MXU_COLS) // LANES) * 2 + np.tile(_PAIR_SLOT, MXU_COLS // LANES)
    block_diag = jnp.asarray(head_of_lane[:, None] == head_of_lane[None, :], bf16)
    (cos_a, sin_a), (cos_bq, sin_bq), (cos_bk, sin_bk) = _rope_tables(S)

    table_spec = pl.BlockSpec((ROW_TILE, LANES), lambda i, b: (i, 0))
    q_out = pl.BlockSpec((1, ROW_TILE, WIDTH), lambda i, b: (b, i, 0))
    k_out = pl.BlockSpec((1, ROW_TILE, KV_WIDTH), lambda i, b: (b, i, 0))
    vat_out = pl.BlockSpec((1, ROW_TILE // KEY_CHUNK, N_KV * VT_ROWS, KEY_CHUNK), lambda i, b: (b, i, 0, 0))
    vbt_out = pl.BlockSpec((1, sub, N_KV * VT_ROWS, Q_TILE), lambda i, b: (b, i, 0, 0))
    qa, ka, vat, qb, kb, vbt = pl.pallas_call(
        _proj_kernel,
        grid=(n_row_tiles, B),
        in_specs=[pl.BlockSpec((1, ROW_TILE, D), lambda i, b: (b, i, 0)),
                  _const_spec((1, D)), _const_spec(w_p.shape), _const_spec(gqk.shape), _const_spec((MXU_COLS, MXU_COLS)),
                  table_spec, table_spec, table_spec, table_spec, table_spec, table_spec],
        out_specs=[q_out, k_out, vat_out, q_out, k_out, vbt_out],
        out_shape=[jax.ShapeDtypeStruct((B, S, WIDTH), bf16), jax.ShapeDtypeStruct((B, S, KV_WIDTH), bf16),
                   jax.ShapeDtypeStruct((B, S // KEY_CHUNK, N_KV * VT_ROWS, KEY_CHUNK), bf16),
                   jax.ShapeDtypeStruct((B, S, WIDTH), bf16), jax.ShapeDtypeStruct((B, S, KV_WIDTH), bf16),
                   jax.ShapeDtypeStruct((B, n_q_tiles, N_KV * VT_ROWS, Q_TILE), bf16)],
        compiler_params=pltpu.CompilerParams(dimension_semantics=("arbitrary", "arbitrary"),
                                             vmem_limit_bytes=VMEM_LIMIT),
        name="proj_rope",
    )(x, norm_mix_pre, w_p, gqk, block_diag, cos_a, sin_a, cos_bq, sin_bq, cos_bk, sin_bk)

    gain_a = jnp.broadcast_to(group_norm_a[0][:, None], (WIDTH, Q_TILE))
    gain_b = jnp.broadcast_to(group_norm_b[0][:, None], (WIDTH, Q_TILE))
    sink_cols = jnp.repeat(sink_b[0] * LOG2E, Q_TILE).reshape(N_KV, GROUP * Q_TILE)
    key_row = np.arange(Q_TILE)[:, None]
    q_col = np.arange(GROUP * Q_TILE)[None, :] % Q_TILE
    neg = np.float32(-np.inf)
    band_bias = jnp.asarray(np.stack([np.where(q_col <= key_row, np.float32(0), neg),
                                      np.where(key_row <= q_col, np.float32(0), neg),
                                      np.full((Q_TILE, GROUP * Q_TILE), neg)]).astype(np.float32))

    step_rows = TILES_PER_STEP * Q_TILE
    n_steps = S // step_rows
    last_step = B * n_steps - 1
    out_spec = pl.BlockSpec((1, step_rows, 2 * WIDTH), lambda b, t: (b, t, 0))
    q_spec = pl.BlockSpec((1, step_rows, WIDTH), lambda b, t: (b, t, 0))

    def next_step(b, t):
        nxt = jnp.minimum(b * n_steps + t + 1, last_step)
        return nxt // n_steps, nxt % n_steps

    def next_q_index(b, t):
        nb, nt = next_step(b, t)
        return nb, nt * TILES_PER_STEP, 0

    def band_tile(t, j):
        return jnp.clip(t * TILES_PER_STEP - 1 + j, 0, n_q_tiles - 1)

    def k_band_spec(j):
        return pl.BlockSpec((1, Q_TILE, KV_WIDTH), lambda b, t: (b, band_tile(t, j), 0))

    def v_band_spec(j):
        return pl.BlockSpec((1, 1, N_KV * VT_ROWS, Q_TILE), lambda b, t: (b, band_tile(t, j), 0, 0))

    n_band = TILES_PER_STEP + 2
    bias_block = (1, Q_TILE, GROUP * Q_TILE)

    def fixed_bias_spec(i):
        return pl.BlockSpec(bias_block, lambda b, t: (i, 0, 0), pipeline_mode=pl.Buffered(1))

    mixer = pl.pallas_call(
        _attn_kernel,
        grid=(B, n_steps),
        in_specs=[q_spec,
                  pl.BlockSpec((1, Q_TILE, WIDTH), next_q_index),
                  pl.BlockSpec((1, S, KV_WIDTH), lambda b, t: (b, 0, 0)),
                  pl.BlockSpec((1, S, KV_WIDTH), lambda b, t: (next_step(b, t)[0], 0, 0)),
                  pl.BlockSpec((1, S // KEY_CHUNK, N_KV * VT_ROWS, KEY_CHUNK), lambda b, t: (b, 0, 0, 0)),
                  _const_spec((WIDTH, Q_TILE)),
                  q_spec, *[k_band_spec(j) for j in range(n_band)], *[v_band_spec(j) for j in range(n_band)],
                  pl.BlockSpec(bias_block, lambda b, t: (jnp.where(t == 0, 2, 0), 0, 0)),
                  pl.BlockSpec(bias_block, lambda b, t: (jnp.where(t == n_steps - 1, 2, 1), 0, 0)),
                  fixed_bias_spec(0), fixed_bias_spec(1),
                  _const_spec(sink_cols.shape), _const_spec((WIDTH, Q_TILE))],
        out_specs=out_spec,
        out_shape=jax.ShapeDtypeStruct((B, S, 2 * WIDTH), bf16),
        scratch_shapes=[pltpu.VMEM((N_KV, S, GROUP * Q_TILE), f32),
                        pltpu.VMEM((2 * N_KV, 1, GROUP * Q_TILE), f32),
                        pltpu.VMEM((N_KV, 3 * Q_TILE, GROUP * Q_TILE), f32)],
        compiler_params=pltpu.CompilerParams(dimension_semantics=("arbitrary", "arbitrary"),
                                             vmem_limit_bytes=VMEM_LIMIT),
        name="attention",
    )(qa, qa, ka, ka, vat, gain_a, qb, *([kb] * n_band), *([vbt] * n_band),
      band_bias, band_bias, band_bias, band_bias, sink_cols, gain_b)

    wg = w_gate[0].astype(bf16)
    wu = w_up[0].astype(bf16)
    wd = w_down[0].astype(bf16)
    wo = w_out[0].astype(bf16)
    row_spec = pl.BlockSpec((FFN_ROW_TILE, D), lambda i: (i, 0))
    mix_spec = pl.BlockSpec((FFN_ROW_TILE, 2 * WIDTH), lambda i: (i, 0))
    out = pl.pallas_call(
        _post_kernel,
        grid=(B * S // FFN_ROW_TILE,),
        in_specs=[row_spec, mix_spec, _const_spec(wo.shape), _const_spec((1, D)), _const_spec((1, D)),
                  _const_spec(wg.shape), _const_spec(wu.shape), _const_spec(wd.shape), _const_spec((1, D))],
        out_specs=row_spec,
        out_shape=jax.ShapeDtypeStruct((B * S, D), f32),
        scratch_shapes=[pltpu.VMEM((FFN_ROW_TILE, D), f32)],
        compiler_params=pltpu.CompilerParams(dimension_semantics=("arbitrary",), vmem_limit_bytes=VMEM_LIMIT),
        name="outproj_ffn",
    )(x.reshape(B * S, D), mixer.reshape(B * S, 2 * WIDTH), wo,
      norm_mix_post, norm_ffn_pre, wg, wu, wd, norm_ffn_post)
    return out.reshape(B, S, D)
```

```python
import math

import numpy as np
import jax
import jax.numpy as jnp
from jax import lax
from jax.experimental import pallas as pl
from jax.experimental.pallas import tpu as pltpu

D_MODEL = 1024
HEAD_DIM = 64
N_HEADS = 8
N_KV = 2
GROUP = N_HEADS // N_KV
WIDTH = N_HEADS * HEAD_DIM
KV_WIDTH = N_KV * HEAD_DIM
WINDOW = 128
GRID_W = 64
ROPE_THETA = 10000.0
EPS = 1e-6
D_FF = 2816
LOG2E = math.log2(math.e)
SCORE_SCALE = HEAD_DIM ** -0.5 * LOG2E

LANES = 128
MXU_COLS = 256
ONES_ROWS = 16
VT_ROWS = HEAD_DIM + ONES_ROWS
ROW_TILE = 512
FFN_ROW_TILE = 1024
Q_TILE = 128
KEY_CHUNK = 256
TILES_PER_STEP = 8
EARLY_CHUNKS = 2
EARLY_LAG = 2
WINDOW_SLOTS = (0, 0, 6, 15)
FF_CHUNK = 256
VMEM_LIMIT = 56 * 1024 * 1024

_PAIR_SLOT = (np.arange(LANES) // (HEAD_DIM // 2)) % 2
_PAIR_DIM = (np.arange(LANES) // HEAD_DIM) * (HEAD_DIM // 2) + np.arange(LANES) % (HEAD_DIM // 2)


def _rms(x, gain):
    return x * lax.rsqrt(jnp.mean(x * x, axis=-1, keepdims=True) + EPS) * gain


def _rope_block(t, cos, sin_signed):
    return t * cos + pltpu.roll(t, LANES // 2, 1) * sin_signed


def _proj_kernel(x_ref, gpre_ref, w_ref, gqk_ref, bd_ref,
                 cos_a_ref, sin_a_ref, cos_bq_ref, sin_bq_ref, cos_bk_ref, sin_bk_ref,
                 qa_ref, ka_ref, vat_ref, qb_ref, kb_ref, vbt_ref):
    x = x_ref[0]
    h = _rms(x, gpre_ref[...]).astype(jnp.bfloat16)
    tm = x.shape[0]
    half = w_ref.shape[1] // 2
    n_qk = (WIDTH + KV_WIDTH) // LANES
    proj_1 = jnp.dot(h, w_ref[:, :half], preferred_element_type=jnp.float32)
    proj_2 = jnp.dot(h, w_ref[:, half:], preferred_element_type=jnp.float32)

    cos_a = cos_a_ref[...]
    sin_a = sin_a_ref[...]
    for j0 in range(0, n_qk, MXU_COLS // LANES):
        nb = min(MXU_COLS // LANES, n_qk - j0)
        t = proj_1[:, j0 * LANES:(j0 + nb) * LANES]
        ssq = jnp.dot((t * t).astype(jnp.bfloat16), bd_ref[:nb * LANES, :nb * LANES],
                      preferred_element_type=jnp.float32)
        y = t * lax.rsqrt(ssq * (1.0 / HEAD_DIM) + EPS) * gqk_ref[:, j0 * LANES:(j0 + nb) * LANES]
        for j in range(j0, j0 + nb):
            r = _rope_block(y[:, (j - j0) * LANES:(j - j0 + 1) * LANES], cos_a, sin_a).astype(jnp.bfloat16)
            if j < n_qk - 1:
                qa_ref[0, :, j * LANES:(j + 1) * LANES] = r
            else:
                ka_ref[0] = r

    kb_ref[0] = _rope_block(proj_1[:, n_qk * LANES:], cos_bk_ref[...], sin_bk_ref[...]).astype(jnp.bfloat16)
    for j in range(WIDTH // LANES):
        t = proj_2[:, j * LANES:(j + 1) * LANES]
        qb_ref[0, :, j * LANES:(j + 1) * LANES] = _rope_block(t, cos_bq_ref[...], sin_bq_ref[...]).astype(jnp.bfloat16)

    for idx, (vt_ref, width) in enumerate(((vat_ref, KEY_CHUNK), (vbt_ref, Q_TILE))):
        o = WIDTH + idx * KV_WIDTH
        vt = proj_2[:, o:o + KV_WIDTH].T.astype(jnp.bfloat16)
        ones = jnp.ones((ONES_ROWS, width), jnp.bfloat16)
        for blk in range(tm // width):
            for kvh in range(N_KV):
                base = kvh * VT_ROWS
                vt_ref[0, blk, base:base + HEAD_DIM, :] = vt[kvh * HEAD_DIM:(kvh + 1) * HEAD_DIM,
                                                             blk * width:(blk + 1) * width]
                vt_ref[0, blk, base + HEAD_DIM:base + VT_ROWS, :] = ones


def _masked_queries(q_ref, row0, kvh):
    lane = lax.broadcasted_iota(jnp.int32, (Q_TILE, LANES), 1)
    keep = (lane // (HEAD_DIM // 2)) % 2 == kvh
    blocks = [jnp.where(keep, q_ref[0, row0:row0 + Q_TILE, j * LANES:(j + 1) * LANES],
                        jnp.zeros((), jnp.bfloat16)).T for j in range(GROUP)]
    return jnp.concatenate(blocks, axis=1)


def _group_norm_store(head_outs, gain_ref, o_ref, row0, group):
    full = jnp.concatenate(head_outs, axis=0)
    ms = jnp.mean(full * full, axis=0, keepdims=True)
    normed = (full * lax.rsqrt(ms + EPS) * gain_ref[...]).T.astype(o_ref.dtype)
    o_ref[0, row0:row0 + Q_TILE, group * WIDTH:(group + 1) * WIDTH] = normed


def _split_heads(out):
    return [out[:, j * Q_TILE:(j + 1) * Q_TILE] for j in range(GROUP)]


def _attn_kernel(*refs):
    n_band = TILES_PER_STEP + 2
    q_ref, qn_ref, k_ref, kn_ref, vt_ref, gain_a_ref, qb_ref = refs[:7]
    kb_refs = refs[7:7 + n_band]
    vb_refs = refs[7 + n_band:7 + 2 * n_band]
    (bias_lo_edge_ref, bias_hi_edge_ref, bias_lo_ref, bias_hi_ref,
     sink_ref, gain_b_ref, o_ref, s_ref, m_ref, sw_ref) = refs[7 + 2 * n_band:]
    n_chunks = k_ref.shape[1] // KEY_CHUNK
    n_cols = GROUP * Q_TILE
    neg_inf = jnp.full((1, n_cols), -jnp.inf, jnp.float32)

    def scores(c, m, qx, keys_ref, buf):
        start = pl.multiple_of(c * KEY_CHUNK, KEY_CHUNK)
        s = jnp.dot(keys_ref[0, pl.ds(start, KEY_CHUNK), :], qx,
                    preferred_element_type=jnp.float32)
        s_ref[buf, pl.ds(start, KEY_CHUNK), :] = s
        return jnp.maximum(m, jnp.max(s, axis=0, keepdims=True))

    def weighted(c, acc, m, buf):
        start = pl.multiple_of(c * KEY_CHUNK, KEY_CHUNK)
        p = jnp.exp2(s_ref[buf, pl.ds(start, KEY_CHUNK), :] - m).astype(jnp.bfloat16)
        vt = vt_ref[0, c, buf * VT_ROWS:(buf + 1) * VT_ROWS, :]
        return acc + jnp.dot(vt, p, preferred_element_type=jnp.float32)

    early_after = n_chunks - 1 - EARLY_LAG
    assert EARLY_CHUNKS - 1 <= early_after

    @pl.when((pl.program_id(0) == 0) & (pl.program_id(1) == 0))
    def _():
        qx = _masked_queries(q_ref, 0, 0)
        m_early = lax.fori_loop(0, early_after + 1, lambda c, m: scores(c, m, qx, k_ref, 0), neg_inf)
        m_ref[0] = m_early
        m_ref[1] = lax.fori_loop(early_after + 1, n_chunks, lambda c, m: scores(c, m, qx, k_ref, 0), m_early)

    window_plan = {}
    for slot, event in zip(WINDOW_SLOTS, (("scores", 0), ("scores", 1), ("weighted", 0), ("weighted", 1))):
        window_plan.setdefault(min(slot * n_chunks // 8, N_KV * n_chunks - 1), []).append(event)

    for r in range(TILES_PER_STEP):
        row0 = r * Q_TILE
        last = r == TILES_PER_STEP - 1
        band_k = kb_refs[r:r + 3]
        band_v = vb_refs[r:r + 3]
        lo_ref = bias_lo_edge_ref if r == 0 else bias_lo_ref
        hi_ref = bias_hi_edge_ref if last else bias_hi_ref
        window_max = [None] * N_KV
        window_outs = [None] * N_HEADS
        head_outs = [None] * N_HEADS
        stages = ((1, q_ref, row0, k_ref),
                  (0, qn_ref, 0, kn_ref) if last else (0, q_ref, row0 + Q_TILE, k_ref))
        for kvh, (nxt, nq_ref, nrow0, nk_ref) in enumerate(stages):
            qx = _masked_queries(nq_ref, nrow0, nxt)
            m_early = m_ref[2 * kvh]
            m = m_ref[2 * kvh + 1]
            m_next = neg_inf
            acc_early = jnp.zeros((VT_ROWS, n_cols), jnp.float32)
            acc = jnp.zeros((VT_ROWS, n_cols), jnp.float32)
            for c in range(n_chunks):
                for kind, wk in window_plan.get(kvh * n_chunks + c, ()):
                    if kind == "scores":
                        window_max[wk] = _window_scores(qb_ref, row0, band_k, lo_ref, hi_ref, sink_ref, wk, sw_ref)
                    else:
                        window_outs[wk * GROUP:(wk + 1) * GROUP] = _window_weighted(
                            band_v, sink_ref, wk, sw_ref, window_max[wk])
                m_next = scores(c, m_next, qx, nk_ref, nxt)
                if c == early_after:
                    m_ref[2 * nxt] = m_next
                if c < EARLY_CHUNKS:
                    acc_early = weighted(c, acc_early, m_early, kvh)
                else:
                    acc = weighted(c, acc, m, kvh)
            m_ref[2 * nxt + 1] = m_next
            acc = acc + acc_early * jnp.exp2(m_early - m)
            head_outs[kvh * GROUP:(kvh + 1) * GROUP] = _split_heads(acc[:HEAD_DIM] / acc[HEAD_DIM:HEAD_DIM + 1])
        _group_norm_store(window_outs, gain_b_ref, o_ref, row0, 1)
        _group_norm_store(head_outs, gain_a_ref, o_ref, row0, 0)


def _window_scores(q_ref, row0, k_refs, bias_lo_ref, bias_hi_ref, sink_ref, kvh, sw_ref):
    qx = _masked_queries(q_ref, row0, kvh)
    k_band = jnp.concatenate([k_ref[0] for k_ref in k_refs], axis=0)
    s_band = jnp.dot(k_band, qx, preferred_element_type=jnp.float32)
    biases = (bias_lo_ref, None, bias_hi_ref)
    m = sink_ref[kvh:kvh + 1, :]
    for i, bias_ref in enumerate(biases):
        s = s_band[i * Q_TILE:(i + 1) * Q_TILE]
        if bias_ref is not None:
            s = s + bias_ref[0]
        sw_ref[kvh, i * Q_TILE:(i + 1) * Q_TILE, :] = s
        m = jnp.maximum(m, jnp.max(s, axis=0, keepdims=True))
    return m


def _window_weighted(v_refs, sink_ref, kvh, sw_ref, m):
    p = jnp.exp2(sw_ref[kvh] - m).astype(jnp.bfloat16)
    rs = slice(kvh * VT_ROWS, (kvh + 1) * VT_ROWS)
    vt = jnp.concatenate([v_ref[0, 0, rs, :] for v_ref in v_refs], axis=1)
    acc = jnp.dot(vt, p, preferred_element_type=jnp.float32)
    den = acc[HEAD_DIM:HEAD_DIM + 1] + jnp.exp2(sink_ref[kvh:kvh + 1, :] - m)
    return _split_heads(acc[:HEAD_DIM] / den)


def _post_kernel(x_ref, mix_ref, wo_ref, gpost_ref, gffn_ref, wg_ref, wu_ref, wd_ref, gfpost_ref,
                 o_ref, acc_ref):
    mixed = jnp.dot(mix_ref[...], wo_ref[...], preferred_element_type=jnp.float32)
    x1 = x_ref[...] + _rms(mixed, gpost_ref[...])
    h = _rms(x1, gffn_ref[...]).astype(jnp.bfloat16)
    acc_ref[...] = jnp.zeros_like(acc_ref)
    for c in range(D_FF // FF_CHUNK):
        cols = slice(c * FF_CHUNK, (c + 1) * FF_CHUNK)
        g = jnp.dot(h, wg_ref[:, cols], preferred_element_type=jnp.float32)
        u = jnp.dot(h, wu_ref[:, cols], preferred_element_type=jnp.float32)
        a = (g / (1.0 + jnp.exp(-g)) * u).astype(jnp.bfloat16)
        acc_ref[...] += jnp.dot(a, wd_ref[cols, :], preferred_element_type=jnp.float32)
    o_ref[...] = x1 + _rms(acc_ref[...], gfpost_ref[...])


def _rope_tables(seq_len):
    rows = seq_len // GRID_W
    t = np.arange(seq_len, dtype=np.float32)
    row = np.repeat(np.arange(rows, dtype=np.float32), GRID_W)
    col = np.tile(np.arange(GRID_W, dtype=np.float32), rows)
    ax_pairs = HEAD_DIM // 4
    freq_ax = jnp.asarray(ROPE_THETA, jnp.float32) ** (-jnp.arange(ax_pairs, dtype=jnp.float32) / ax_pairs)
    ang_axial = jnp.concatenate([row[:, None] * freq_ax[None, :], col[:, None] * freq_ax[None, :]], axis=-1)
    n_pairs = HEAD_DIM // 2
    freq_1d = jnp.asarray(ROPE_THETA, jnp.float32) ** (-jnp.arange(n_pairs, dtype=jnp.float32) / n_pairs)
    ang_1d = t[:, None] * freq_1d[None, :]

    def tables(ang, scale):
        cos = jnp.cos(ang) * scale
        sin = jnp.sin(ang) * scale
        return jnp.tile(cos, (1, 4)), jnp.concatenate([-sin, -sin, sin, sin], axis=-1)

    return tables(ang_axial, 1.0), tables(ang_1d, SCORE_SCALE), tables(ang_1d, 1.0)


def _const_spec(shape):
    return pl.BlockSpec(shape, lambda *_: (0,) * len(shape), pipeline_mode=pl.Buffered(1))


def kernel(x, norm_mix_pre, w_in, q_norm_a, k_norm_a, sink_b, group_norm_a, group_norm_b, w_out,
           norm_mix_post, norm_ffn_pre, w_gate, w_up, w_down, norm_ffn_post):
    B, S, D = x.shape
    assert D == D_MODEL and S % ROW_TILE == 0 and S % GRID_W == 0 and S % (TILES_PER_STEP * Q_TILE) == 0
    f32, bf16 = jnp.float32, jnp.bfloat16
    n_row_tiles = S // ROW_TILE
    n_q_tiles = S // Q_TILE
    sub = ROW_TILE // Q_TILE

    w = w_in[0]
    q_cols = np.concatenate([(j + GROUP * _PAIR_SLOT) * HEAD_DIM + _PAIR_DIM for j in range(GROUP)])
    k_cols = _PAIR_SLOT * HEAD_DIM + _PAIR_DIM
    o_ka, o_va, o_qb, o_kb, o_vb = WIDTH, WIDTH + KV_WIDTH, WIDTH + 2 * KV_WIDTH, 2 * WIDTH + 2 * KV_WIDTH, 2 * WIDTH + 3 * KV_WIDTH
    col_idx = np.concatenate([q_cols, o_ka + k_cols, o_kb + k_cols, o_qb + q_cols,
                              o_va + np.arange(KV_WIDTH), o_vb + np.arange(KV_WIDTH)])
    w_p = w[:, col_idx].astype(bf16)
    gqk = jnp.concatenate([jnp.tile(q_norm_a[0][_PAIR_DIM] * SCORE_SCALE, GROUP), k_norm_a[0][_PAIR_DIM]])[None, :]
    head_of_lane = (np.arange(MXU_COLS) // LANES) * 2 + np.tile(_PAIR_SLOT, MXU_COLS // LANES)
    block_diag = jnp.asarray(head_of_lane[:, None] == head_of_lane[None, :], bf16)
    (cos_a, sin_a), (cos_bq, sin_bq), (cos_bk, sin_bk) = _rope_tables(S)

    table_spec = pl.BlockSpec((ROW_TILE, LANES), lambda i, b: (i, 0))
    q_out = pl.BlockSpec((1, ROW_TILE, WIDTH), lambda i, b: (b, i, 0))
    k_out = pl.BlockSpec((1, ROW_TILE, KV_WIDTH), lambda i, b: (b, i, 0))
    vat_out = pl.BlockSpec((1, ROW_TILE // KEY_CHUNK, N_KV * VT_ROWS, KEY_CHUNK), lambda i, b: (b, i, 0, 0))
    vbt_out = pl.BlockSpec((1, sub, N_KV * VT_ROWS, Q_TILE), lambda i, b: (b, i, 0, 0))
    qa, ka, vat, qb, kb, vbt = pl.pallas_call(
        _proj_kernel,
        grid=(n_row_tiles, B),
        in_specs=[pl.BlockSpec((1, ROW_TILE, D), lambda i, b: (b, i, 0)),
                  _const_spec((1, D)), _const_spec(w_p.shape), _const_spec(gqk.shape), _const_spec((MXU_COLS, MXU_COLS)),
                  table_spec, table_spec, table_spec, table_spec, table_spec, table_spec],
        out_specs=[q_out, k_out, vat_out, q_out, k_out, vbt_out],
        out_shape=[jax.ShapeDtypeStruct((B, S, WIDTH), bf16), jax.ShapeDtypeStruct((B, S, KV_WIDTH), bf16),
                   jax.ShapeDtypeStruct((B, S // KEY_CHUNK, N_KV * VT_ROWS, KEY_CHUNK), bf16),
                   jax.ShapeDtypeStruct((B, S, WIDTH), bf16), jax.ShapeDtypeStruct((B, S, KV_WIDTH), bf16),
                   jax.ShapeDtypeStruct((B, n_q_tiles, N_KV * VT_ROWS, Q_TILE), bf16)],
        compiler_params=pltpu.CompilerParams(dimension_semantics=("arbitrary", "arbitrary"),
                                             vmem_limit_bytes=VMEM_LIMIT),
        name="proj_rope",
    )(x, norm_mix_pre, w_p, gqk, block_diag, cos_a, sin_a, cos_bq, sin_bq, cos_bk, sin_bk)

    gain_a = jnp.broadcast_to(group_norm_a[0][:, None], (WIDTH, Q_TILE))
    gain_b = jnp.broadcast_to(group_norm_b[0][:, None], (WIDTH, Q_TILE))
    sink_cols = jnp.repeat(sink_b[0] * LOG2E, Q_TILE).reshape(N_KV, GROUP * Q_TILE)
    key_row = np.arange(Q_TILE)[:, None]
    q_col = np.arange(GROUP * Q_TILE)[None, :] % Q_TILE
    neg = np.float32(-np.inf)
    band_bias = jnp.asarray(np.stack([np.where(q_col <= key_row, np.float32(0), neg),
                                      np.where(key_row <= q_col, np.float32(0), neg),
                                      np.full((Q_TILE, GROUP * Q_TILE), neg)]).astype(np.float32))

    step_rows = TILES_PER_STEP * Q_TILE
    n_steps = S // step_rows
    last_step = B * n_steps - 1
    out_spec = pl.BlockSpec((1, step_rows, 2 * WIDTH), lambda b, t: (b, t, 0))
    q_spec = pl.BlockSpec((1, step_rows, WIDTH), lambda b, t: (b, t, 0))

    def next_step(b, t):
        nxt = jnp.minimum(b * n_steps + t + 1, last_step)
        return nxt // n_steps, nxt % n_steps

    def next_q_index(b, t):
        nb, nt = next_step(b, t)
        return nb, nt * TILES_PER_STEP, 0

    def band_tile(t, j):
        return jnp.clip(t * TILES_PER_STEP - 1 + j, 0, n_q_tiles - 1)

    def k_band_spec(j):
        return pl.BlockSpec((1, Q_TILE, KV_WIDTH), lambda b, t: (b, band_tile(t, j), 0))

    def v_band_spec(j):
        return pl.BlockSpec((1, 1, N_KV * VT_ROWS, Q_TILE), lambda b, t: (b, band_tile(t, j), 0, 0))

    n_band = TILES_PER_STEP + 2
    bias_block = (1, Q_TILE, GROUP * Q_TILE)

    def fixed_bias_spec(i):
        return pl.BlockSpec(bias_block, lambda b, t: (i, 0, 0), pipeline_mode=pl.Buffered(1))

    mixer = pl.pallas_call(
        _attn_kernel,
        grid=(B, n_steps),
        in_specs=[q_spec,
                  pl.BlockSpec((1, Q_TILE, WIDTH), next_q_index),
                  pl.BlockSpec((1, S, KV_WIDTH), lambda b, t: (b, 0, 0)),
                  pl.BlockSpec((1, S, KV_WIDTH), lambda b, t: (next_step(b, t)[0], 0, 0)),
                  pl.BlockSpec((1, S // KEY_CHUNK, N_KV * VT_ROWS, KEY_CHUNK), lambda b, t: (b, 0, 0, 0)),
                  _const_spec((WIDTH, Q_TILE)),
                  q_spec, *[k_band_spec(j) for j in range(n_band)], *[v_band_spec(j) for j in range(n_band)],
                  pl.BlockSpec(bias_block, lambda b, t: (jnp.where(t == 0, 2, 0), 0, 0)),
                  pl.BlockSpec(bias_block, lambda b, t: (jnp.where(t == n_steps - 1, 2, 1), 0, 0)),
                  fixed_bias_spec(0), fixed_bias_spec(1),
                  _const_spec(sink_cols.shape), _const_spec((WIDTH, Q_TILE))],
        out_specs=out_spec,
        out_shape=jax.ShapeDtypeStruct((B, S, 2 * WIDTH), bf16),
        scratch_shapes=[pltpu.VMEM((N_KV, S, GROUP * Q_TILE), f32),
                        pltpu.VMEM((2 * N_KV, 1, GROUP * Q_TILE), f32),
                        pltpu.VMEM((N_KV, 3 * Q_TILE, GROUP * Q_TILE), f32)],
        compiler_params=pltpu.CompilerParams(dimension_semantics=("arbitrary", "arbitrary"),
                                             vmem_limit_bytes=VMEM_LIMIT),
        name="attention",
    )(qa, qa, ka, ka, vat, gain_a, qb, *([kb] * n_band), *([vbt] * n_band),
      band_bias, band_bias, band_bias, band_bias, sink_cols, gain_b)

    wg = w_gate[0].astype(bf16)
    wu = w_up[0].astype(bf16)
    wd = w_down[0].astype(bf16)
    wo = w_out[0].astype(bf16)
    row_spec = pl.BlockSpec((FFN_ROW_TILE, D), lambda i: (i, 0))
    mix_spec = pl.BlockSpec((FFN_ROW_TILE, 2 * WIDTH), lambda i: (i, 0))
    out = pl.pallas_call(
        _post_kernel,
        grid=(B * S // FFN_ROW_TILE,),
        in_specs=[row_spec, mix_spec, _const_spec(wo.shape), _const_spec((1, D)), _const_spec((1, D)),
                  _const_spec(wg.shape), _const_spec(wu.shape), _const_spec(wd.shape), _const_spec((1, D))],
        out_specs=row_spec,
        out_shape=jax.ShapeDtypeStruct((B * S, D), f32),
        scratch_shapes=[pltpu.VMEM((FFN_ROW_TILE, D), f32)],
        compiler_params=pltpu.CompilerParams(dimension_semantics=("arbitrary",), vmem_limit_bytes=VMEM_LIMIT),
        name="outproj_ffn",
    )(x.reshape(B * S, D), mixer.reshape(B * S, 2 * WIDTH), wo,
      norm_mix_post, norm_ffn_pre, wg, wu, wd, norm_ffn_post)
    return out.reshape(B, S, D)
```

```python
import math

import numpy as np
import jax
import jax.numpy as jnp
from jax import lax
from jax.experimental import pallas as pl
from jax.experimental.pallas import tpu as pltpu

D_MODEL = 1024
HEAD_DIM = 64
N_HEADS = 8
N_KV = 2
GROUP = N_HEADS // N_KV
WIDTH = N_HEADS * HEAD_DIM
KV_WIDTH = N_KV * HEAD_DIM
WINDOW = 128
GRID_W = 64
ROPE_THETA = 10000.0
EPS = 1e-6
D_FF = 2816
LOG2E = math.log2(math.e)
SCORE_SCALE = HEAD_DIM ** -0.5 * LOG2E

LANES = 128
MXU_COLS = 256
ONES_ROWS = 16
VT_ROWS = HEAD_DIM + ONES_ROWS
ROW_TILE = 1024
FFN_ROW_TILE = 1024
Q_TILE = 128
KEY_CHUNK = 256
TILES_PER_STEP = 4
EARLY_CHUNKS = 2
EARLY_LAG = 2
WINDOW_SLOTS = (0, 0, 6, 15)
FF_CHUNK = 256
VMEM_LIMIT = 56 * 1024 * 1024

_PAIR_SLOT = (np.arange(LANES) // (HEAD_DIM // 2)) % 2
_PAIR_DIM = (np.arange(LANES) // HEAD_DIM) * (HEAD_DIM // 2) + np.arange(LANES) % (HEAD_DIM // 2)


def _rms(x, gain):
    return x * lax.rsqrt(jnp.mean(x * x, axis=-1, keepdims=True) + EPS) * gain


def _rope_block(t, cos, sin_signed):
    return t * cos + pltpu.roll(t, LANES // 2, 1) * sin_signed


def _proj_kernel(x_ref, gpre_ref, w_ref, gqk_ref, bd_ref,
                 cos_a_ref, sin_a_ref, cos_bq_ref, sin_bq_ref, cos_bk_ref, sin_bk_ref,
                 qa_ref, ka_ref, vat_ref, qb_ref, kb_ref, vbt_ref):
    x = x_ref[0]
    h = _rms(x, gpre_ref[...]).astype(jnp.bfloat16)
    tm = x.shape[0]
    half = w_ref.shape[1] // 2
    n_qk = (WIDTH + KV_WIDTH) // LANES
    proj_1 = jnp.dot(h, w_ref[:, :half], preferred_element_type=jnp.float32)
    proj_2 = jnp.dot(h, w_ref[:, half:], preferred_element_type=jnp.float32)

    cos_a = cos_a_ref[...]
    sin_a = sin_a_ref[...]
    for j0 in range(0, n_qk, MXU_COLS // LANES):
        nb = min(MXU_COLS // LANES, n_qk - j0)
        t = proj_1[:, j0 * LANES:(j0 + nb) * LANES]
        ssq = jnp.dot((t * t).astype(jnp.bfloat16), bd_ref[:nb * LANES, :nb * LANES],
                      preferred_element_type=jnp.float32)
        y = t * lax.rsqrt(ssq * (1.0 / HEAD_DIM) + EPS) * gqk_ref[:, j0 * LANES:(j0 + nb) * LANES]
        for j in range(j0, j0 + nb):
            r = _rope_block(y[:, (j - j0) * LANES:(j - j0 + 1) * LANES], cos_a, sin_a).astype(jnp.bfloat16)
            if j < n_qk - 1:
                qa_ref[0, :, j * LANES:(j + 1) * LANES] = r
            else:
                ka_ref[0] = r

    kb_ref[0] = _rope_block(proj_1[:, n_qk * LANES:], cos_bk_ref[...], sin_bk_ref[...]).astype(jnp.bfloat16)
    for j in range(WIDTH // LANES):
        t = proj_2[:, j * LANES:(j + 1) * LANES]
        qb_ref[0, :, j * LANES:(j + 1) * LANES] = _rope_block(t, cos_bq_ref[...], sin_bq_ref[...]).astype(jnp.bfloat16)

    for idx, (vt_ref, width) in enumerate(((vat_ref, KEY_CHUNK), (vbt_ref, Q_TILE))):
        o = WIDTH + idx * KV_WIDTH
        vt = proj_2[:, o:o + KV_WIDTH].T.astype(jnp.bfloat16)
        ones = jnp.ones((ONES_ROWS, width), jnp.bfloat16)
        for blk in range(tm // width):
            for kvh in range(N_KV):
                base = kvh * VT_ROWS
                vt_ref[0, blk, base:base + HEAD_DIM, :] = vt[kvh * HEAD_DIM:(kvh + 1) * HEAD_DIM,
                                                             blk * width:(blk + 1) * width]
                vt_ref[0, blk, base + HEAD_DIM:base + VT_ROWS, :] = ones


def _masked_queries(q_ref, row0, kvh):
    lane = lax.broadcasted_iota(jnp.int32, (Q_TILE, LANES), 1)
    keep = (lane // (HEAD_DIM // 2)) % 2 == kvh
    blocks = [jnp.where(keep, q_ref[0, row0:row0 + Q_TILE, j * LANES:(j + 1) * LANES],
                        jnp.zeros((), jnp.bfloat16)).T for j in range(GROUP)]
    return jnp.concatenate(blocks, axis=1)


def _group_norm_store(head_outs, gain_ref, o_ref, row0, group):
    full = jnp.concatenate(head_outs, axis=0)
    ms = jnp.mean(full * full, axis=0, keepdims=True)
    normed = (full * lax.rsqrt(ms + EPS) * gain_ref[...]).T.astype(o_ref.dtype)
    o_ref[0, row0:row0 + Q_TILE, group * WIDTH:(group + 1) * WIDTH] = normed


def _split_heads(out):
    return [out[:, j * Q_TILE:(j + 1) * Q_TILE] for j in range(GROUP)]


def _attn_kernel(*refs):
    n_band = TILES_PER_STEP + 2
    q_ref, qn_ref, k_ref, kn_ref, vt_ref, gain_a_ref, qb_ref = refs[:7]
    kb_refs = refs[7:7 + n_band]
    vb_refs = refs[7 + n_band:7 + 2 * n_band]
    (bias_lo_edge_ref, bias_hi_edge_ref, bias_lo_ref, bias_hi_ref,
     sink_ref, gain_b_ref, o_ref, s_ref, m_ref, sw_ref) = refs[7 + 2 * n_band:]
    n_chunks = k_ref.shape[1] // KEY_CHUNK
    n_cols = GROUP * Q_TILE
    neg_inf = jnp.full((1, n_cols), -jnp.inf, jnp.float32)

    def scores(c, m, qx, keys_ref, buf):
        start = pl.multiple_of(c * KEY_CHUNK, KEY_CHUNK)
        s = jnp.dot(keys_ref[0, pl.ds(start, KEY_CHUNK), :], qx,
                    preferred_element_type=jnp.float32)
        s_ref[buf, pl.ds(start, KEY_CHUNK), :] = s
        return jnp.maximum(m, jnp.max(s, axis=0, keepdims=True))

    def weighted(c, acc, m, buf):
        start = pl.multiple_of(c * KEY_CHUNK, KEY_CHUNK)
        p = jnp.exp2(s_ref[buf, pl.ds(start, KEY_CHUNK), :] - m).astype(jnp.bfloat16)
        vt = vt_ref[0, c, buf * VT_ROWS:(buf + 1) * VT_ROWS, :]
        return acc + jnp.dot(vt, p, preferred_element_type=jnp.float32)

    early_after = n_chunks - 1 - EARLY_LAG
    assert EARLY_CHUNKS - 1 <= early_after

    @pl.when((pl.program_id(0) == 0) & (pl.program_id(1) == 0))
    def _():
        qx = _masked_queries(q_ref, 0, 0)
        m_early = lax.fori_loop(0, early_after + 1, lambda c, m: scores(c, m, qx, k_ref, 0), neg_inf)
        m_ref[0] = m_early
        m_ref[1] = lax.fori_loop(early_after + 1, n_chunks, lambda c, m: scores(c, m, qx, k_ref, 0), m_early)

    window_plan = {}
    for slot, event in zip(WINDOW_SLOTS, (("scores", 0), ("scores", 1), ("weighted", 0), ("weighted", 1))):
        window_plan.setdefault(min(slot * n_chunks // 8, N_KV * n_chunks - 1), []).append(event)

    for r in range(TILES_PER_STEP):
        row0 = r * Q_TILE
        last = r == TILES_PER_STEP - 1
        band_k = kb_refs[r:r + 3]
        band_v = vb_refs[r:r + 3]
        lo_ref = bias_lo_edge_ref if r == 0 else bias_lo_ref
        hi_ref = bias_hi_edge_ref if last else bias_hi_ref
        window_max = [None] * N_KV
        window_outs = [None] * N_HEADS
        head_outs = [None] * N_HEADS
        stages = ((1, q_ref, row0, k_ref),
                  (0, qn_ref, 0, kn_ref) if last else (0, q_ref, row0 + Q_TILE, k_ref))
        for kvh, (nxt, nq_ref, nrow0, nk_ref) in enumerate(stages):
            qx = _masked_queries(nq_ref, nrow0, nxt)
            m_early = m_ref[2 * kvh]
            m = m_ref[2 * kvh + 1]
            m_next = neg_inf
            acc_early = jnp.zeros((VT_ROWS, n_cols), jnp.float32)
            acc = jnp.zeros((VT_ROWS, n_cols), jnp.float32)
            for c in range(n_chunks):
                for kind, wk in window_plan.get(kvh * n_chunks + c, ()):
                    if kind == "scores":
                        window_max[wk] = _window_scores(qb_ref, row0, band_k, lo_ref, hi_ref, sink_ref, wk, sw_ref)
                    else:
                        window_outs[wk * GROUP:(wk + 1) * GROUP] = _window_weighted(
                            band_v, sink_ref, wk, sw_ref, window_max[wk])
                m_next = scores(c, m_next, qx, nk_ref, nxt)
                if c == early_after:
                    m_ref[2 * nxt] = m_next
                if c < EARLY_CHUNKS:
                    acc_early = weighted(c, acc_early, m_early, kvh)
                else:
                    acc = weighted(c, acc, m, kvh)
            m_ref[2 * nxt + 1] = m_next
            acc = acc + acc_early * jnp.exp2(m_early - m)
            head_outs[kvh * GROUP:(kvh + 1) * GROUP] = _split_heads(acc[:HEAD_DIM] / acc[HEAD_DIM:HEAD_DIM + 1])
        _group_norm_store(window_outs, gain_b_ref, o_ref, row0, 1)
        _group_norm_store(head_outs, gain_a_ref, o_ref, row0, 0)


def _window_scores(q_ref, row0, k_refs, bias_lo_ref, bias_hi_ref, sink_ref, kvh, sw_ref):
    qx = _masked_queries(q_ref, row0, kvh)
    k_band = jnp.concatenate([k_ref[0] for k_ref in k_refs], axis=0)
    s_band = jnp.dot(k_band, qx, preferred_element_type=jnp.float32)
    biases = (bias_lo_ref, None, bias_hi_ref)
    m = sink_ref[kvh:kvh + 1, :]
    for i, bias_ref in enumerate(biases):
        s = s_band[i * Q_TILE:(i + 1) * Q_TILE]
        if bias_ref is not None:
            s = s + bias_ref[0]
        sw_ref[kvh, i * Q_TILE:(i + 1) * Q_TILE, :] = s
        m = jnp.maximum(m, jnp.max(s, axis=0, keepdims=True))
    return m


def _window_weighted(v_refs, sink_ref, kvh, sw_ref, m):
    p = jnp.exp2(sw_ref[kvh] - m).astype(jnp.bfloat16)
    rs = slice(kvh * VT_ROWS, (kvh + 1) * VT_ROWS)
    vt = jnp.concatenate([v_ref[0, 0, rs, :] for v_ref in v_refs], axis=1)
    acc = jnp.dot(vt, p, preferred_element_type=jnp.float32)
    den = acc[HEAD_DIM:HEAD_DIM + 1] + jnp.exp2(sink_ref[kvh:kvh + 1, :] - m)
    return _split_heads(acc[:HEAD_DIM] / den)


def _post_kernel(x_ref, mix_ref, wo_ref, gpost_ref, gffn_ref, wg_ref, wu_ref, wd_ref, gfpost_ref,
                 o_ref, acc_ref):
    mixed = jnp.dot(mix_ref[...], wo_ref[...], preferred_element_type=jnp.float32)
    x1 = x_ref[...] + _rms(mixed, gpost_ref[...])
    h = _rms(x1, gffn_ref[...]).astype(jnp.bfloat16)
    acc_ref[...] = jnp.zeros_like(acc_ref)
    for c in range(D_FF // FF_CHUNK):
        cols = slice(c * FF_CHUNK, (c + 1) * FF_CHUNK)
        g = jnp.dot(h, wg_ref[:, cols], preferred_element_type=jnp.float32)
        u = jnp.dot(h, wu_ref[:, cols], preferred_element_type=jnp.float32)
        a = (g / (1.0 + jnp.exp(-g)) * u).astype(jnp.bfloat16)
        acc_ref[...] += jnp.dot(a, wd_ref[cols, :], preferred_element_type=jnp.float32)
    o_ref[...] = x1 + _rms(acc_ref[...], gfpost_ref[...])


def _rope_tables(seq_len):
    rows = seq_len // GRID_W
    t = np.arange(seq_len, dtype=np.float32)
    row = np.repeat(np.arange(rows, dtype=np.float32), GRID_W)
    col = np.tile(np.arange(GRID_W, dtype=np.float32), rows)
    ax_pairs = HEAD_DIM // 4
    freq_ax = jnp.asarray(ROPE_THETA, jnp.float32) ** (-jnp.arange(ax_pairs, dtype=jnp.float32) / ax_pairs)
    ang_axial = jnp.concatenate([row[:, None] * freq_ax[None, :], col[:, None] * freq_ax[None, :]], axis=-1)
    n_pairs = HEAD_DIM // 2
    freq_1d = jnp.asarray(ROPE_THETA, jnp.float32) ** (-jnp.arange(n_pairs, dtype=jnp.float32) / n_pairs)
    ang_1d = t[:, None] * freq_1d[None, :]

    def tables(ang, scale):
        cos = jnp.cos(ang) * scale
        sin = jnp.sin(ang) * scale
        return jnp.tile(cos, (1, 4)), jnp.concatenate([-sin, -sin, sin, sin], axis=-1)

    return tables(ang_axial, 1.0), tables(ang_1d, SCORE_SCALE), tables(ang_1d, 1.0)


def _const_spec(shape):
    return pl.BlockSpec(shape, lambda *_: (0,) * len(shape), pipeline_mode=pl.Buffered(1))


def kernel(x, norm_mix_pre, w_in, q_norm_a, k_norm_a, sink_b, group_norm_a, group_norm_b, w_out,
           norm_mix_post, norm_ffn_pre, w_gate, w_up, w_down, norm_ffn_post):
    B, S, D = x.shape
    assert D == D_MODEL and S % ROW_TILE == 0 and S % GRID_W == 0 and S % (TILES_PER_STEP * Q_TILE) == 0
    f32, bf16 = jnp.float32, jnp.bfloat16
    n_row_tiles = S // ROW_TILE
    n_q_tiles = S // Q_TILE
    sub = ROW_TILE // Q_TILE

    w = w_in[0]
    q_cols = np.concatenate([(j + GROUP * _PAIR_SLOT) * HEAD_DIM + _PAIR_DIM for j in range(GROUP)])
    k_cols = _PAIR_SLOT * HEAD_DIM + _PAIR_DIM
    o_ka, o_va, o_qb, o_kb, o_vb = WIDTH, WIDTH + KV_WIDTH, WIDTH + 2 * KV_WIDTH, 2 * WIDTH + 2 * KV_WIDTH, 2 * WIDTH + 3 * KV_WIDTH
    col_idx = np.concatenate([q_cols, o_ka + k_cols, o_kb + k_cols, o_qb + q_cols,
                              o_va + np.arange(KV_WIDTH), o_vb + np.arange(KV_WIDTH)])
    w_p = w[:, col_idx].astype(bf16)
    gqk = jnp.concatenate([jnp.tile(q_norm_a[0][_PAIR_DIM] * SCORE_SCALE, GROUP), k_norm_a[0][_PAIR_DIM]])[None, :]
    head_of_lane = (np.arange(MXU_COLS) // LANES) * 2 + np.tile(_PAIR_SLOT, MXU_COLS // LANES)
    block_diag = jnp.asarray(head_of_lane[:, None] == head_of_lane[None, :], bf16)
    (cos_a, sin_a), (cos_bq, sin_bq), (cos_bk, sin_bk) = _rope_tables(S)

    table_spec = pl.BlockSpec((ROW_TILE, LANES), lambda i, b: (i, 0))
    q_out = pl.BlockSpec((1, ROW_TILE, WIDTH), lambda i, b: (b, i, 0))
    k_out = pl.BlockSpec((1, ROW_TILE, KV_WIDTH), lambda i, b: (b, i, 0))
    vat_out = pl.BlockSpec((1, ROW_TILE // KEY_CHUNK, N_KV * VT_ROWS, KEY_CHUNK), lambda i, b: (b, i, 0, 0))
    vbt_out = pl.BlockSpec((1, sub, N_KV * VT_ROWS, Q_TILE), lambda i, b: (b, i, 0, 0))
    qa, ka, vat, qb, kb, vbt = pl.pallas_call(
        _proj_kernel,
        grid=(n_row_tiles, B),
        in_specs=[pl.BlockSpec((1, ROW_TILE, D), lambda i, b: (b, i, 0)),
                  _const_spec((1, D)), _const_spec(w_p.shape), _const_spec(gqk.shape), _const_spec((MXU_COLS, MXU_COLS)),
                  table_spec, table_spec, table_spec, table_spec, table_spec, table_spec],
        out_specs=[q_out, k_out, vat_out, q_out, k_out, vbt_out],
        out_shape=[jax.ShapeDtypeStruct((B, S, WIDTH), bf16), jax.ShapeDtypeStruct((B, S, KV_WIDTH), bf16),
                   jax.ShapeDtypeStruct((B, S // KEY_CHUNK, N_KV * VT_ROWS, KEY_CHUNK), bf16),
                   jax.ShapeDtypeStruct((B, S, WIDTH), bf16), jax.ShapeDtypeStruct((B, S, KV_WIDTH), bf16),
                   jax.ShapeDtypeStruct((B, n_q_tiles, N_KV * VT_ROWS, Q_TILE), bf16)],
        compiler_params=pltpu.CompilerParams(dimension_semantics=("arbitrary", "arbitrary"),
                                             vmem_limit_bytes=VMEM_LIMIT),
        name="proj_rope",
    )(x, norm_mix_pre, w_p, gqk, block_diag, cos_a, sin_a, cos_bq, sin_bq, cos_bk, sin_bk)

    gain_a = jnp.broadcast_to(group_norm_a[0][:, None], (WIDTH, Q_TILE))
    gain_b = jnp.broadcast_to(group_norm_b[0][:, None], (WIDTH, Q_TILE))
    sink_cols = jnp.repeat(sink_b[0] * LOG2E, Q_TILE).reshape(N_KV, GROUP * Q_TILE)
    key_row = np.arange(Q_TILE)[:, None]
    q_col = np.arange(GROUP * Q_TILE)[None, :] % Q_TILE
    neg = np.float32(-np.inf)
    band_bias = jnp.asarray(np.stack([np.where(q_col <= key_row, np.float32(0), neg),
                                      np.where(key_row <= q_col, np.float32(0), neg),
                                      np.full((Q_TILE, GROUP * Q_TILE), neg)]).astype(np.float32))

    step_rows = TILES_PER_STEP * Q_TILE
    n_steps = S // step_rows
    last_step = B * n_steps - 1
    out_spec = pl.BlockSpec((1, step_rows, 2 * WIDTH), lambda b, t: (b, t, 0))
    q_spec = pl.BlockSpec((1, step_rows, WIDTH), lambda b, t: (b, t, 0))

    def next_step(b, t):
        nxt = jnp.minimum(b * n_steps + t + 1, last_step)
        return nxt // n_steps, nxt % n_steps

    def next_q_index(b, t):
        nb, nt = next_step(b, t)
        return nb, nt * TILES_PER_STEP, 0

    def band_tile(t, j):
        return jnp.clip(t * TILES_PER_STEP - 1 + j, 0, n_q_tiles - 1)

    def k_band_spec(j):
        return pl.BlockSpec((1, Q_TILE, KV_WIDTH), lambda b, t: (b, band_tile(t, j), 0))

    def v_band_spec(j):
        return pl.BlockSpec((1, 1, N_KV * VT_ROWS, Q_TILE), lambda b, t: (b, band_tile(t, j), 0, 0))

    n_band = TILES_PER_STEP + 2
    bias_block = (1, Q_TILE, GROUP * Q_TILE)

    def fixed_bias_spec(i):
        return pl.BlockSpec(bias_block, lambda b, t: (i, 0, 0), pipeline_mode=pl.Buffered(1))

    mixer = pl.pallas_call(
        _attn_kernel,
        grid=(B, n_steps),
        in_specs=[q_spec,
                  pl.BlockSpec((1, Q_TILE, WIDTH), next_q_index),
                  pl.BlockSpec((1, S, KV_WIDTH), lambda b, t: (b, 0, 0)),
                  pl.BlockSpec((1, S, KV_WIDTH), lambda b, t: (next_step(b, t)[0], 0, 0)),
                  pl.BlockSpec((1, S // KEY_CHUNK, N_KV * VT_ROWS, KEY_CHUNK), lambda b, t: (b, 0, 0, 0)),
                  _const_spec((WIDTH, Q_TILE)),
                  q_spec, *[k_band_spec(j) for j in range(n_band)], *[v_band_spec(j) for j in range(n_band)],
                  pl.BlockSpec(bias_block, lambda b, t: (jnp.where(t == 0, 2, 0), 0, 0)),
                  pl.BlockSpec(bias_block, lambda b, t: (jnp.where(t == n_steps - 1, 2, 1), 0, 0)),
                  fixed_bias_spec(0), fixed_bias_spec(1),
                  _const_spec(sink_cols.shape), _const_spec((WIDTH, Q_TILE))],
        out_specs=out_spec,
        out_shape=jax.ShapeDtypeStruct((B, S, 2 * WIDTH), bf16),
        scratch_shapes=[pltpu.VMEM((N_KV, S, GROUP * Q_TILE), f32),
                        pltpu.VMEM((2 * N_KV, 1, GROUP * Q_TILE), f32),
                        pltpu.VMEM((N_KV, 3 * Q_TILE, GROUP * Q_TILE), f32)],
        compiler_params=pltpu.CompilerParams(dimension_semantics=("arbitrary", "arbitrary"),
                                             vmem_limit_bytes=VMEM_LIMIT),
        name="attention",
    )(qa, qa, ka, ka, vat, gain_a, qb, *([kb] * n_band), *([vbt] * n_band),
      band_bias, band_bias, band_bias, band_bias, sink_cols, gain_b)

    wg = w_gate[0].astype(bf16)
    wu = w_up[0].astype(bf16)
    wd = w_down[0].astype(bf16)
    wo = w_out[0].astype(bf16)
    row_spec = pl.BlockSpec((FFN_ROW_TILE, D), lambda i: (i, 0))
    mix_spec = pl.BlockSpec((FFN_ROW_TILE, 2 * WIDTH), lambda i: (i, 0))
    out = pl.pallas_call(
        _post_kernel,
        grid=(B * S // FFN_ROW_TILE,),
        in_specs=[row_spec, mix_spec, _const_spec(wo.shape), _const_spec((1, D)), _const_spec((1, D)),
                  _const_spec(wg.shape), _const_spec(wu.shape), _const_spec(wd.shape), _const_spec((1, D))],
        out_specs=row_spec,
        out_shape=jax.ShapeDtypeStruct((B * S, D), f32),
        scratch_shapes=[pltpu.VMEM((FFN_ROW_TILE, D), f32)],
        compiler_params=pltpu.CompilerParams(dimension_semantics=("arbitrary",), vmem_limit_bytes=VMEM_LIMIT),
        name="outproj_ffn",
    )(x.reshape(B * S, D), mixer.reshape(B * S, 2 * WIDTH), wo,
      norm_mix_post, norm_ffn_pre, wg, wu, wd, norm_ffn_post)
    return out.reshape(B, S, D)
```

```python
import math

import numpy as np
import jax
import jax.numpy as jnp
from jax import lax
from jax.experimental import pallas as pl
from jax.experimental.pallas import tpu as pltpu

D_MODEL = 1024
HEAD_DIM = 64
N_HEADS = 8
N_KV = 2
GROUP = N_HEADS // N_KV
WIDTH = N_HEADS * HEAD_DIM
KV_WIDTH = N_KV * HEAD_DIM
WINDOW = 128
GRID_W = 64
ROPE_THETA = 10000.0
EPS = 1e-6
D_FF = 2816
LOG2E = math.log2(math.e)
SCORE_SCALE = HEAD_DIM ** -0.5 * LOG2E

LANES = 128
MXU_COLS = 256
ONES_ROWS = 16
VT_ROWS = HEAD_DIM + ONES_ROWS
ROW_TILE = 1024
FFN_ROW_TILE = 1024
Q_TILE = 128
KEY_CHUNK = 256
TILES_PER_STEP = 4
EARLY_CHUNKS = 2
EARLY_LAG = 2
WINDOW_SLOTS = (0, 0, 6, 15)
FF_CHUNK = 256
VMEM_LIMIT = 56 * 1024 * 1024

_PAIR_SLOT = (np.arange(LANES) // (HEAD_DIM // 2)) % 2
_PAIR_DIM = (np.arange(LANES) // HEAD_DIM) * (HEAD_DIM // 2) + np.arange(LANES) % (HEAD_DIM // 2)


def _rms(x, gain):
    return x * lax.rsqrt(jnp.mean(x * x, axis=-1, keepdims=True) + EPS) * gain


def _rope_block(t, cos, sin_signed):
    return t * cos + pltpu.roll(t, LANES // 2, 1) * sin_signed


def _proj_kernel(x_ref, gpre_ref, w_ref, gqk_ref, bd_ref,
                 cos_a_ref, sin_a_ref, cos_bq_ref, sin_bq_ref, cos_bk_ref, sin_bk_ref,
                 qa_ref, ka_ref, vat_ref, qb_ref, kb_ref, vbt_ref):
    x = x_ref[0]
    h = _rms(x, gpre_ref[...]).astype(jnp.bfloat16)
    tm = x.shape[0]
    half = w_ref.shape[1] // 2
    n_qk = (WIDTH + KV_WIDTH) // LANES
    proj_1 = jnp.dot(h, w_ref[:, :half], preferred_element_type=jnp.float32)
    proj_2 = jnp.dot(h, w_ref[:, half:], preferred_element_type=jnp.float32)

    cos_a = cos_a_ref[...]
    sin_a = sin_a_ref[...]
    for j0 in range(0, n_qk, MXU_COLS // LANES):
        nb = min(MXU_COLS // LANES, n_qk - j0)
        t = proj_1[:, j0 * LANES:(j0 + nb) * LANES]
        ssq = jnp.dot((t * t).astype(jnp.bfloat16), bd_ref[:nb * LANES, :nb * LANES],
                      preferred_element_type=jnp.float32)
        y = t * lax.rsqrt(ssq * (1.0 / HEAD_DIM) + EPS) * gqk_ref[:, j0 * LANES:(j0 + nb) * LANES]
        for j in range(j0, j0 + nb):
            r = _rope_block(y[:, (j - j0) * LANES:(j - j0 + 1) * LANES], cos_a, sin_a).astype(jnp.bfloat16)
            if j < n_qk - 1:
                qa_ref[0, :, j * LANES:(j + 1) * LANES] = r
            else:
                ka_ref[0] = r

    kb_ref[0] = _rope_block(proj_1[:, n_qk * LANES:], cos_bk_ref[...], sin_bk_ref[...]).astype(jnp.bfloat16)
    for j in range(WIDTH // LANES):
        t = proj_2[:, j * LANES:(j + 1) * LANES]
        qb_ref[0, :, j * LANES:(j + 1) * LANES] = _rope_block(t, cos_bq_ref[...], sin_bq_ref[...]).astype(jnp.bfloat16)

    for idx, (vt_ref, width) in enumerate(((vat_ref, KEY_CHUNK), (vbt_ref, Q_TILE))):
        o = WIDTH + idx * KV_WIDTH
        vt = proj_2[:, o:o + KV_WIDTH].T.astype(jnp.bfloat16)
        ones = jnp.ones((ONES_ROWS, width), jnp.bfloat16)
        for blk in range(tm // width):
            for kvh in range(N_KV):
                base = kvh * VT_ROWS
                vt_ref[0, blk, base:base + HEAD_DIM, :] = vt[kvh * HEAD_DIM:(kvh + 1) * HEAD_DIM,
                                                             blk * width:(blk + 1) * width]
                vt_ref[0, blk, base + HEAD_DIM:base + VT_ROWS, :] = ones


def _masked_queries(q_ref, row0, kvh):
    lane = lax.broadcasted_iota(jnp.int32, (Q_TILE, LANES), 1)
    keep = (lane // (HEAD_DIM // 2)) % 2 == kvh
    blocks = [jnp.where(keep, q_ref[0, row0:row0 + Q_TILE, j * LANES:(j + 1) * LANES],
                        jnp.zeros((), jnp.bfloat16)).T for j in range(GROUP)]
    return jnp.concatenate(blocks, axis=1)


def _group_norm_store(head_outs, gain_ref, o_ref, row0, group):
    full = jnp.concatenate(head_outs, axis=0)
    ms = jnp.mean(full * full, axis=0, keepdims=True)
    normed = (full * lax.rsqrt(ms + EPS) * gain_ref[...]).T.astype(o_ref.dtype)
    o_ref[0, row0:row0 + Q_TILE, group * WIDTH:(group + 1) * WIDTH] = normed


def _split_heads(out):
    return [out[:, j * Q_TILE:(j + 1) * Q_TILE] for j in range(GROUP)]


def _attn_kernel(*refs):
    n_band = TILES_PER_STEP + 2
    q_ref, qn_ref, k_ref, kn_ref, vt_ref, gain_a_ref, qb_ref = refs[:7]
    kb_refs = refs[7:7 + n_band]
    vb_refs = refs[7 + n_band:7 + 2 * n_band]
    (bias_lo_edge_ref, bias_hi_edge_ref, bias_lo_ref, bias_hi_ref,
     sink_ref, gain_b_ref, o_ref, s_ref, m_ref, sw_ref) = refs[7 + 2 * n_band:]
    n_chunks = k_ref.shape[1] // KEY_CHUNK
    n_cols = GROUP * Q_TILE
    neg_inf = jnp.full((1, n_cols), -jnp.inf, jnp.float32)

    def scores(c, m, qx, keys_ref, buf):
        start = pl.multiple_of(c * KEY_CHUNK, KEY_CHUNK)
        s = jnp.dot(keys_ref[0, pl.ds(start, KEY_CHUNK), :], qx,
                    preferred_element_type=jnp.float32)
        s_ref[buf, pl.ds(start, KEY_CHUNK), :] = s
        return jnp.maximum(m, jnp.max(s, axis=0, keepdims=True))

    def weighted(c, acc, m, buf):
        start = pl.multiple_of(c * KEY_CHUNK, KEY_CHUNK)
        p = jnp.exp2(s_ref[buf, pl.ds(start, KEY_CHUNK), :] - m).astype(jnp.bfloat16)
        vt = vt_ref[0, c, buf * VT_ROWS:(buf + 1) * VT_ROWS, :]
        return acc + jnp.dot(vt, p, preferred_element_type=jnp.float32)

    early_after = n_chunks - 1 - EARLY_LAG
    assert EARLY_CHUNKS - 1 <= early_after

    @pl.when((pl.program_id(0) == 0) & (pl.program_id(1) == 0))
    def _():
        qx = _masked_queries(q_ref, 0, 0)
        m_early = lax.fori_loop(0, early_after + 1, lambda c, m: scores(c, m, qx, k_ref, 0), neg_inf)
        m_ref[0] = m_early
        m_ref[1] = lax.fori_loop(early_after + 1, n_chunks, lambda c, m: scores(c, m, qx, k_ref, 0), m_early)

    window_plan = {}
    for slot, event in zip(WINDOW_SLOTS, (("scores", 0), ("scores", 1), ("weighted", 0), ("weighted", 1))):
        window_plan.setdefault(min(slot * n_chunks // 8, N_KV * n_chunks - 1), []).append(event)

    for r in range(TILES_PER_STEP):
        row0 = r * Q_TILE
        last = r == TILES_PER_STEP - 1
        band_k = kb_refs[r:r + 3]
        band_v = vb_refs[r:r + 3]
        lo_ref = bias_lo_edge_ref if r == 0 else bias_lo_ref
        hi_ref = bias_hi_edge_ref if last else bias_hi_ref
        window_max = [None] * N_KV
        window_outs = [None] * N_HEADS
        head_outs = [None] * N_HEADS
        stages = ((1, q_ref, row0, k_ref),
                  (0, qn_ref, 0, kn_ref) if last else (0, q_ref, row0 + Q_TILE, k_ref))
        for kvh, (nxt, nq_ref, nrow0, nk_ref) in enumerate(stages):
            qx = _masked_queries(nq_ref, nrow0, nxt)
            m_early = m_ref[2 * kvh]
            m = m_ref[2 * kvh + 1]
            m_next = neg_inf
            acc_early = jnp.zeros((VT_ROWS, n_cols), jnp.float32)
            acc = jnp.zeros((VT_ROWS, n_cols), jnp.float32)
            for c in range(n_chunks):
                for kind, wk in window_plan.get(kvh * n_chunks + c, ()):
                    if kind == "scores":
                        window_max[wk] = _window_scores(qb_ref, row0, band_k, lo_ref, hi_ref, sink_ref, wk, sw_ref)
                    else:
                        window_outs[wk * GROUP:(wk + 1) * GROUP] = _window_weighted(
                            band_v, sink_ref, wk, sw_ref, window_max[wk])
                m_next = scores(c, m_next, qx, nk_ref, nxt)
                if c == early_after:
                    m_ref[2 * nxt] = m_next
                if c < EARLY_CHUNKS:
                    acc_early = weighted(c, acc_early, m_early, kvh)
                else:
                    acc = weighted(c, acc, m, kvh)
            m_ref[2 * nxt + 1] = m_next
            acc = acc + acc_early * jnp.exp2(m_early - m)
            head_outs[kvh * GROUP:(kvh + 1) * GROUP] = _split_heads(acc[:HEAD_DIM] / acc[HEAD_DIM:HEAD_DIM + 1])
        _group_norm_store(window_outs, gain_b_ref, o_ref, row0, 1)
        _group_norm_store(head_outs, gain_a_ref, o_ref, row0, 0)


def _window_scores(q_ref, row0, k_refs, bias_lo_ref, bias_hi_ref, sink_ref, kvh, sw_ref):
    qx = _masked_queries(q_ref, row0, kvh)
    k_band = jnp.concatenate([k_ref[0] for k_ref in k_refs], axis=0)
    s_band = jnp.dot(k_band, qx, preferred_element_type=jnp.float32)
    biases = (bias_lo_ref, None, bias_hi_ref)
    m = sink_ref[kvh:kvh + 1, :]
    for i, bias_ref in enumerate(biases):
        s = s_band[i * Q_TILE:(i + 1) * Q_TILE]
        if bias_ref is not None:
            s = s + bias_ref[0]
        sw_ref[kvh, i * Q_TILE:(i + 1) * Q_TILE, :] = s
        m = jnp.maximum(m, jnp.max(s, axis=0, keepdims=True))
    return m


def _window_weighted(v_refs, sink_ref, kvh, sw_ref, m):
    p = jnp.exp2(sw_ref[kvh] - m).astype(jnp.bfloat16)
    rs = slice(kvh * VT_ROWS, (kvh + 1) * VT_ROWS)
    vt = jnp.concatenate([v_ref[0, 0, rs, :] for v_ref in v_refs], axis=1)
    acc = jnp.dot(vt, p, preferred_element_type=jnp.float32)
    den = acc[HEAD_DIM:HEAD_DIM + 1] + jnp.exp2(sink_ref[kvh:kvh + 1, :] - m)
    return _split_heads(acc[:HEAD_DIM] / den)


def _post_kernel(x_ref, mix_ref, wo_ref, gpost_ref, gffn_ref, wg_ref, wu_ref, wd_ref, gfpost_ref,
                 o_ref, acc_ref, h_ref):
    mixed = jnp.dot(mix_ref[...], wo_ref[...], preferred_element_type=jnp.float32)
    x1 = x_ref[...] + _rms(mixed, gpost_ref[...])
    o_ref[...] = x1
    h_ref[...] = _rms(x1, gffn_ref[...]).astype(jnp.bfloat16)
    acc_ref[...] = jnp.zeros_like(acc_ref)
    for c in range(D_FF // FF_CHUNK):
        cols = slice(c * FF_CHUNK, (c + 1) * FF_CHUNK)
        g = jnp.dot(h_ref[...], wg_ref[:, cols], preferred_element_type=jnp.float32)
        u = jnp.dot(h_ref[...], wu_ref[:, cols], preferred_element_type=jnp.float32)
        a = (g / (1.0 + jnp.exp(-g)) * u).astype(jnp.bfloat16)
        acc_ref[...] += jnp.dot(a, wd_ref[cols, :], preferred_element_type=jnp.float32)
    o_ref[...] += _rms(acc_ref[...], gfpost_ref[...])


def _rope_tables(seq_len):
    rows = seq_len // GRID_W
    t = np.arange(seq_len, dtype=np.float32)
    row = np.repeat(np.arange(rows, dtype=np.float32), GRID_W)
    col = np.tile(np.arange(GRID_W, dtype=np.float32), rows)
    ax_pairs = HEAD_DIM // 4
    freq_ax = jnp.asarray(ROPE_THETA, jnp.float32) ** (-jnp.arange(ax_pairs, dtype=jnp.float32) / ax_pairs)
    ang_axial = jnp.concatenate([row[:, None] * freq_ax[None, :], col[:, None] * freq_ax[None, :]], axis=-1)
    n_pairs = HEAD_DIM // 2
    freq_1d = jnp.asarray(ROPE_THETA, jnp.float32) ** (-jnp.arange(n_pairs, dtype=jnp.float32) / n_pairs)
    ang_1d = t[:, None] * freq_1d[None, :]

    def tables(ang, scale):
        cos = jnp.cos(ang) * scale
        sin = jnp.sin(ang) * scale
        return jnp.tile(cos, (1, 4)), jnp.concatenate([-sin, -sin, sin, sin], axis=-1)

    return tables(ang_axial, 1.0), tables(ang_1d, SCORE_SCALE), tables(ang_1d, 1.0)


def _const_spec(shape):
    return pl.BlockSpec(shape, lambda *_: (0,) * len(shape), pipeline_mode=pl.Buffered(1))


def kernel(x, norm_mix_pre, w_in, q_norm_a, k_norm_a, sink_b, group_norm_a, group_norm_b, w_out,
           norm_mix_post, norm_ffn_pre, w_gate, w_up, w_down, norm_ffn_post):
    B, S, D = x.shape
    assert D == D_MODEL and S % ROW_TILE == 0 and S % GRID_W == 0 and S % (TILES_PER_STEP * Q_TILE) == 0
    f32, bf16 = jnp.float32, jnp.bfloat16
    n_row_tiles = S // ROW_TILE
    n_q_tiles = S // Q_TILE
    sub = ROW_TILE // Q_TILE

    w = w_in[0]
    q_cols = np.concatenate([(j + GROUP * _PAIR_SLOT) * HEAD_DIM + _PAIR_DIM for j in range(GROUP)])
    k_cols = _PAIR_SLOT * HEAD_DIM + _PAIR_DIM
    o_ka, o_va, o_qb, o_kb, o_vb = WIDTH, WIDTH + KV_WIDTH, WIDTH + 2 * KV_WIDTH, 2 * WIDTH + 2 * KV_WIDTH, 2 * WIDTH + 3 * KV_WIDTH
    col_idx = np.concatenate([q_cols, o_ka + k_cols, o_kb + k_cols, o_qb + q_cols,
                              o_va + np.arange(KV_WIDTH), o_vb + np.arange(KV_WIDTH)])
    w_p = w[:, col_idx].astype(bf16)
    gqk = jnp.concatenate([jnp.tile(q_norm_a[0][_PAIR_DIM] * SCORE_SCALE, GROUP), k_norm_a[0][_PAIR_DIM]])[None, :]
    head_of_lane = (np.arange(MXU_COLS) // LANES) * 2 + np.tile(_PAIR_SLOT, MXU_COLS // LANES)
    block_diag = jnp.asarray(head_of_lane[:, None] == head_of_lane[None, :], bf16)
    (cos_a, sin_a), (cos_bq, sin_bq), (cos_bk, sin_bk) = _rope_tables(S)

    table_spec = pl.BlockSpec((ROW_TILE, LANES), lambda i, b: (i, 0))
    q_out = pl.BlockSpec((1, ROW_TILE, WIDTH), lambda i, b: (b, i, 0))
    k_out = pl.BlockSpec((1, ROW_TILE, KV_WIDTH), lambda i, b: (b, i, 0))
    vat_out = pl.BlockSpec((1, ROW_TILE // KEY_CHUNK, N_KV * VT_ROWS, KEY_CHUNK), lambda i, b: (b, i, 0, 0))
    vbt_out = pl.BlockSpec((1, sub, N_KV * VT_ROWS, Q_TILE), lambda i, b: (b, i, 0, 0))
    qa, ka, vat, qb, kb, vbt = pl.pallas_call(
        _proj_kernel,
        grid=(n_row_tiles, B),
        in_specs=[pl.BlockSpec((1, ROW_TILE, D), lambda i, b: (b, i, 0)),
                  _const_spec((1, D)), _const_spec(w_p.shape), _const_spec(gqk.shape), _const_spec((MXU_COLS, MXU_COLS)),
                  table_spec, table_spec, table_spec, table_spec, table_spec, table_spec],
        out_specs=[q_out, k_out, vat_out, q_out, k_out, vbt_out],
        out_shape=[jax.ShapeDtypeStruct((B, S, WIDTH), bf16), jax.ShapeDtypeStruct((B, S, KV_WIDTH), bf16),
                   jax.ShapeDtypeStruct((B, S // KEY_CHUNK, N_KV * VT_ROWS, KEY_CHUNK), bf16),
                   jax.ShapeDtypeStruct((B, S, WIDTH), bf16), jax.ShapeDtypeStruct((B, S, KV_WIDTH), bf16),
                   jax.ShapeDtypeStruct((B, n_q_tiles, N_KV * VT_ROWS, Q_TILE), bf16)],
        compiler_params=pltpu.CompilerParams(dimension_semantics=("arbitrary", "arbitrary"),
                                             vmem_limit_bytes=VMEM_LIMIT),
        name="proj_rope",
    )(x, norm_mix_pre, w_p, gqk, block_diag, cos_a, sin_a, cos_bq, sin_bq, cos_bk, sin_bk)

    gain_a = jnp.broadcast_to(group_norm_a[0][:, None], (WIDTH, Q_TILE))
    gain_b = jnp.broadcast_to(group_norm_b[0][:, None], (WIDTH, Q_TILE))
    sink_cols = jnp.repeat(sink_b[0] * LOG2E, Q_TILE).reshape(N_KV, GROUP * Q_TILE)
    key_row = np.arange(Q_TILE)[:, None]
    q_col = np.arange(GROUP * Q_TILE)[None, :] % Q_TILE
    neg = np.float32(-np.inf)
    band_bias = jnp.asarray(np.stack([np.where(q_col <= key_row, np.float32(0), neg),
                                      np.where(key_row <= q_col, np.float32(0), neg),
                                      np.full((Q_TILE, GROUP * Q_TILE), neg)]).astype(np.float32))

    step_rows = TILES_PER_STEP * Q_TILE
    n_steps = S // step_rows
    last_step = B * n_steps - 1
    out_spec = pl.BlockSpec((1, step_rows, 2 * WIDTH), lambda b, t: (b, t, 0))
    q_spec = pl.BlockSpec((1, step_rows, WIDTH), lambda b, t: (b, t, 0))

    def next_step(b, t):
        nxt = jnp.minimum(b * n_steps + t + 1, last_step)
        return nxt // n_steps, nxt % n_steps

    def next_q_index(b, t):
        nb, nt = next_step(b, t)
        return nb, nt * TILES_PER_STEP, 0

    def band_tile(t, j):
        return jnp.clip(t * TILES_PER_STEP - 1 + j, 0, n_q_tiles - 1)

    def k_band_spec(j):
        return pl.BlockSpec((1, Q_TILE, KV_WIDTH), lambda b, t: (b, band_tile(t, j), 0))

    def v_band_spec(j):
        return pl.BlockSpec((1, 1, N_KV * VT_ROWS, Q_TILE), lambda b, t: (b, band_tile(t, j), 0, 0))

    n_band = TILES_PER_STEP + 2
    bias_block = (1, Q_TILE, GROUP * Q_TILE)

    def fixed_bias_spec(i):
        return pl.BlockSpec(bias_block, lambda b, t: (i, 0, 0), pipeline_mode=pl.Buffered(1))

    mixer = pl.pallas_call(
        _attn_kernel,
        grid=(B, n_steps),
        in_specs=[q_spec,
                  pl.BlockSpec((1, Q_TILE, WIDTH), next_q_index),
                  pl.BlockSpec((1, S, KV_WIDTH), lambda b, t: (b, 0, 0)),
                  pl.BlockSpec((1, S, KV_WIDTH), lambda b, t: (next_step(b, t)[0], 0, 0)),
                  pl.BlockSpec((1, S // KEY_CHUNK, N_KV * VT_ROWS, KEY_CHUNK), lambda b, t: (b, 0, 0, 0)),
                  _const_spec((WIDTH, Q_TILE)),
                  q_spec, *[k_band_spec(j) for j in range(n_band)], *[v_band_spec(j) for j in range(n_band)],
                  pl.BlockSpec(bias_block, lambda b, t: (jnp.where(t == 0, 2, 0), 0, 0)),
                  pl.BlockSpec(bias_block, lambda b, t: (jnp.where(t == n_steps - 1, 2, 1), 0, 0)),
                  fixed_bias_spec(0), fixed_bias_spec(1),
                  _const_spec(sink_cols.shape), _const_spec((WIDTH, Q_TILE))],
        out_specs=out_spec,
        out_shape=jax.ShapeDtypeStruct((B, S, 2 * WIDTH), bf16),
        scratch_shapes=[pltpu.VMEM((N_KV, S, GROUP * Q_TILE), f32),
                        pltpu.VMEM((2 * N_KV, 1, GROUP * Q_TILE), f32),
                        pltpu.VMEM((N_KV, 3 * Q_TILE, GROUP * Q_TILE), f32)],
        compiler_params=pltpu.CompilerParams(dimension_semantics=("arbitrary", "arbitrary"),
                                             vmem_limit_bytes=VMEM_LIMIT),
        name="attention",
    )(qa, qa, ka, ka, vat, gain_a, qb, *([kb] * n_band), *([vbt] * n_band),
      band_bias, band_bias, band_bias, band_bias, sink_cols, gain_b)

    wg = w_gate[0].astype(bf16)
    wu = w_up[0].astype(bf16)
    wd = w_down[0].astype(bf16)
    wo = w_out[0].astype(bf16)
    row_spec = pl.BlockSpec((FFN_ROW_TILE, D), lambda i: (i, 0))
    mix_spec = pl.BlockSpec((FFN_ROW_TILE, 2 * WIDTH), lambda i: (i, 0))
    out = pl.pallas_call(
        _post_kernel,
        grid=(B * S // FFN_ROW_TILE,),
        in_specs=[row_spec, mix_spec, _const_spec(wo.shape), _const_spec((1, D)), _const_spec((1, D)),
                  _const_spec(wg.shape), _const_spec(wu.shape), _const_spec(wd.shape), _const_spec((1, D))],
        out_specs=row_spec,
        out_shape=jax.ShapeDtypeStruct((B * S, D), f32),
        scratch_shapes=[pltpu.VMEM((FFN_ROW_TILE, D), f32), pltpu.VMEM((FFN_ROW_TILE, D), bf16)],
        compiler_params=pltpu.CompilerParams(dimension_semantics=("arbitrary",), vmem_limit_bytes=VMEM_LIMIT),
        name="outproj_ffn",
    )(x.reshape(B * S, D), mixer.reshape(B * S, 2 * WIDTH), wo,
      norm_mix_post, norm_ffn_pre, wg, wu, wd, norm_ffn_post)
    return out.reshape(B, S, D)
```
